```python
import math
import jax, jax.numpy as jnp
from jax import lax
import numpy as np

D_MODEL = 1024
BATCH = 4
SEQ = 4096
DEPTH = 1
DEC_BATCH = 128
DEC_SEQ = 1
PAST_LEN = 8192
PAGE_SIZE = 128

HEAD_DIM = 128
HEADS_PER_GROUP = 4
DIL_PATTERNS = ((128, 1), (512, 4), (2048, 16))
N_DIL_GROUPS = len(DIL_PATTERNS)
N_ATTN_HEADS = N_DIL_GROUPS * HEADS_PER_GROUP
ATTN_W = N_ATTN_HEADS * HEAD_DIM
ATTN_OUT_W = HEADS_PER_GROUP * HEAD_DIM
QBLOCK = 128
ROPE_THETA = 10000.0
S5_WIDTH = D_MODEL
S5_GROUP_CH = 16
S5_GROUPS = S5_WIDTH // S5_GROUP_CH
S5_STATE = 64
N_EXPERTS = 32
TOP_K = 4
D_EXPERT = D_MODEL
SWIGLU_LIMIT = 7.0
SWIGLU_ALPHA = 1.702
MOE_BLOCK = 128
RMS_EPS = 1e-6
IN_W = 3 * ATTN_W + S5_WIDTH + 2 * D_MODEL
SPLIT_POINTS = (ATTN_W, 2 * ATTN_W, 3 * ATTN_W, 3 * ATTN_W + S5_WIDTH, 3 * ATTN_W + S5_WIDTH + D_MODEL)

kernel_name = "hybrid_dilated_attn_s5_moe_step"


def rmsnorm(x, g):
    xf = x.astype(jnp.float32)
    y = xf * lax.rsqrt(jnp.mean(xf * xf, axis=-1, keepdims=True) + RMS_EPS) * g.astype(jnp.float32)
    return y.astype(x.dtype)


def rope(t, pos):
    half = HEAD_DIM // 2
    freq = ROPE_THETA ** (-jnp.arange(half, dtype=jnp.float32) / half)
    ang = pos.astype(jnp.float32)[:, None] * freq[None, :]
    shape = (ang.shape[0],) + (1,) * (t.ndim - 3) + (half,)
    cos, sin = jnp.cos(ang).reshape(shape), jnp.sin(ang).reshape(shape)
    t1 = t[..., :half].astype(jnp.float32)
    t2 = t[..., half:].astype(jnp.float32)
    return jnp.concatenate([t1 * cos - t2 * sin, t2 * cos + t1 * sin], axis=-1).astype(t.dtype)


def dilated_prompt(q, k, v, dil, span):
    B, S, H, Dh = q.shape
    blk = dil * QBLOCK
    Sp = -(-S // blk) * blk
    M = Sp // dil
    nb = M // QBLOCK

    def to_blocks(t):
        t = jnp.pad(t, ((0, 0), (0, Sp - S), (0, 0), (0, 0)))
        t = t.reshape(B, M, dil, H, Dh).transpose(0, 2, 1, 3, 4)
        return t.reshape(B, dil, nb, QBLOCK, H, Dh)

    def with_prev(t):
        prev = jnp.pad(t, ((0, 0), (0, 0), (1, 0), (0, 0), (0, 0), (0, 0)))[:, :, :-1]
        return jnp.concatenate([prev, t], axis=3)

    qb = to_blocks(q)
    kb = with_prev(to_blocks(k))
    vb = with_prev(to_blocks(v))
    s = jnp.einsum('brnqhd,brnkhd->brnhqk', qb, kb).astype(jnp.float32) * (HEAD_DIM ** -0.5)
    qi = jnp.arange(QBLOCK)[:, None]
    ki = jnp.arange(2 * QBLOCK)[None, :]
    dist = QBLOCK + qi - ki
    band = (dist >= 0) & (dist <= span)
    not_before_start = (jnp.arange(nb)[:, None, None] > 0) | (ki[None] >= QBLOCK)
    valid = band[None] & not_before_start
    s = jnp.where(valid[:, None], s, -jnp.inf)
    mx = jnp.max(s, axis=-1, keepdims=True)
    p = jnp.exp(s - mx)
    den = jnp.sum(p, axis=-1, keepdims=True)
    o = jnp.einsum('brnhqk,brnkhd->brnqhd', p / den, vb.astype(jnp.float32))
    lse = (mx + jnp.log(den))[..., 0]
    o = o.reshape(B, dil, M, H, Dh).transpose(0, 2, 1, 3, 4).reshape(B, Sp, H, Dh)[:, :S]
    lse = lse.transpose(0, 1, 2, 4, 3).reshape(B, dil, M, H).transpose(0, 2, 1, 3).reshape(B, Sp, H)[:, :S]
    return o, lse


def dilated_sample(q, k_new, v_new, cache_kv, dil, span):
    L = cache_kv.shape[1]
    DS = q.shape[1]
    r = L + jnp.arange(DS)[:, None] - dil * jnp.arange(span + 1)[None, :]
    valid = r >= 0
    from_cache = (r < L)[None, :, :, None, None]
    kv_c = cache_kv[:, jnp.clip(r, 0, L - 1)]
    rn = jnp.clip(r - L, 0, DS - 1)
    kg = jnp.where(from_cache, kv_c[:, :, :, 0], k_new[:, rn])
    vg = jnp.where(from_cache, kv_c[:, :, :, 1], v_new[:, rn])
    s = jnp.einsum('bqhd,bqjhd->bqhj', q, kg).astype(jnp.float32) * (HEAD_DIM ** -0.5)
    s = jnp.where(valid[None, :, None, :], s, -jnp.inf)
    mx = jnp.max(s, axis=-1, keepdims=True)
    p = jnp.exp(s - mx)
    den = jnp.sum(p, axis=-1, keepdims=True)
    o = jnp.einsum('bqhj,bqjhd->bqhd', p / den, vg.astype(jnp.float32))
    lse = (mx + jnp.log(den))[..., 0]
    return o, lse


def _linear_combine(left, right):
    a1, b1 = left
    a2, b2 = right
    return a1 * a2, a2 * b1 + b2


def s5_scan(u, a_re, a_im, log_step, b_re, b_im, c_re, c_im, d_skip, h0):
    B, S, _ = u.shape
    f32 = jnp.float32
    ug = u.reshape(B, S, S5_GROUPS, S5_GROUP_CH).astype(f32)
    A = lax.complex(a_re.astype(f32), a_im.astype(f32))
    step = jnp.exp(log_step.astype(f32))[:, None]
    a_bar = jnp.exp(A * step)
    Bc = lax.complex(b_re.astype(f32), b_im.astype(f32))
    b_bar = ((a_bar - 1.0) / A)[..., None] * Bc
    bu = jnp.einsum('gnc,bsgc->bsgn', b_bar, ug)
    if h0 is not None:
        h_prev = lax.complex(h0[0].astype(f32), h0[1].astype(f32))
        bu = bu.at[:, 0].add(a_bar[None] * h_prev)
    a_seq = jnp.broadcast_to(a_bar, (1, S) + a_bar.shape)
    _, h = lax.associative_scan(_linear_combine, (a_seq, bu), axis=1)
    Cc = lax.complex(c_re.astype(f32), c_im.astype(f32))
    y = jnp.real(jnp.einsum('gcn,bsgn->bsgc', Cc, h)) + d_skip.astype(f32) * ug
    return y.reshape(B, S, S5_WIDTH), h[:, -1]


def moe_ffn(h, w_router, b_router, w_gate_up, b_gate_up, w_down, b_down):
    T, D = h.shape
    logits = (h @ w_router).astype(jnp.float32) + b_router.astype(jnp.float32)
    top_val, top_idx = lax.top_k(logits, TOP_K)
    gate = jax.nn.softmax(top_val, axis=-1)
    n_assign = T * TOP_K
    flat_e = top_idx.reshape(-1).astype(jnp.int32)
    flat_tok = jnp.repeat(jnp.arange(T, dtype=jnp.int32), TOP_K)
    flat_w = gate.reshape(-1)
    order = jnp.argsort(flat_e, stable=True)
    se, stok, sw = flat_e[order], flat_tok[order], flat_w[order]
    counts = jnp.bincount(flat_e, length=N_EXPERTS)
    start = jnp.cumsum(counts) - counts
    pcounts = (counts + MOE_BLOCK - 1) // MOE_BLOCK * MOE_BLOCK
    pstart = jnp.cumsum(pcounts) - pcounts
    dest = pstart[se] + jnp.arange(n_assign, dtype=jnp.int32) - start[se]
    n_blocks = -(-n_assign // MOE_BLOCK) + N_EXPERTS
    n_rows = n_blocks * MOE_BLOCK
    row_tok = jnp.zeros((n_rows,), jnp.int32).at[dest].set(stok)
    row_w = jnp.zeros((n_rows,), jnp.float32).at[dest].set(sw)
    row_e = jnp.zeros((n_rows,), jnp.int32).at[dest].set(se)
    blk_e = row_e[::MOE_BLOCK]
    xb = h[row_tok].reshape(n_blocks, MOE_BLOCK, D)

    def expert_block(args):
        xblk, e = args
        gu = xblk @ w_gate_up[e] + b_gate_up[e]
        g, lin = jnp.split(gu, 2, axis=-1)
        g = jnp.minimum(g, SWIGLU_LIMIT)
        lin = jnp.clip(lin, -SWIGLU_LIMIT, SWIGLU_LIMIT)
        act = g * jax.nn.sigmoid(SWIGLU_ALPHA * g) * (lin + 1.0)
        return act @ w_down[e] + b_down[e]

    yb = lax.map(expert_block, (xb, blk_e))
    contrib = yb.reshape(n_rows, D).astype(jnp.float32) * row_w[:, None]
    return jnp.zeros((T, D), jnp.float32).at[row_tok].add(contrib).astype(h.dtype)


def hybrid_layer(x, pos, kv_cache, s5_h0, norm_mix, w_in, q_norm, k_norm, s5_a_re, s5_a_im, s5_log_step,
                 s5_b_re, s5_b_im, s5_c_re, s5_c_im, s5_d, w_glu, b_glu, w_branch_attn, w_branch_s5,
                 w_out, norm_ffn, w_router, b_router, w_gate_up, b_gate_up, w_down, b_down):
    B, S, D = x.shape
    h = rmsnorm(x, norm_mix)
    proj = h @ w_in
    q, k, v, u, g_attn, g_s5 = jnp.split(proj, SPLIT_POINTS, axis=-1)
    hs = (B, S, N_DIL_GROUPS, HEADS_PER_GROUP, HEAD_DIM)
    q = rope(rmsnorm(q.reshape(hs), q_norm[:, None, :]), pos)
    k = rope(rmsnorm(k.reshape(hs), k_norm[:, None, :]), pos)
    v = v.reshape(hs)
    outs, lses, new_kv = [], [], []
    for g, (window, dil) in enumerate(DIL_PATTERNS):
        span = window // dil
        qg, kg, vg = q[:, :, g], k[:, :, g], v[:, :, g]
        if kv_cache is None:
            o, lse = dilated_prompt(qg, kg, vg, dil, span)
            keep = min(window, S)
            new_kv.append(jnp.stack([kg[:, S - keep:], vg[:, S - keep:]], axis=2))
        else:
            o, lse = dilated_sample(qg, kg, vg, kv_cache[g], dil, span)
            new_kv.append(jnp.stack([kg, vg], axis=2))
        outs.append(o)
        lses.append(lse)
    mix_w = jax.nn.softmax(jnp.stack(lses, axis=0), axis=0)
    attn = jnp.einsum('gbsh,gbshd->bshd', mix_w, jnp.stack(outs, axis=0))
    attn = attn.reshape(B, S, ATTN_OUT_W).astype(x.dtype)
    y_s5, h_last = s5_scan(u, s5_a_re, s5_a_im, s5_log_step, s5_b_re, s5_b_im, s5_c_re, s5_c_im, s5_d, s5_h0)
    z = jax.nn.gelu(y_s5).astype(x.dtype)
    s5_out = z * jax.nn.sigmoid(z @ w_glu + b_glu)
    merged = jax.nn.sigmoid(g_attn) * (attn @ w_branch_attn) + jax.nn.sigmoid(g_s5) * (s5_out @ w_branch_s5)
    x = x + merged @ w_out
    h2 = rmsnorm(x, norm_ffn).reshape(B * S, D)
    x = x + moe_ffn(h2, w_router, b_router, w_gate_up, b_gate_up, w_down, b_down).reshape(B, S, D)
    return x, new_kv, jnp.real(h_last).astype(x.dtype), jnp.imag(h_last).astype(x.dtype)


def setup_inputs(seed: int = 0) -> dict:
    key = jax.random.key(seed)
    ks = iter(jax.random.split(key, 40))
    f32 = jnp.float32

    def nrm(shape, scale):
        return scale * jax.random.normal(next(ks), shape, f32)

    lw = [min(w, PAST_LEN) for w, _ in DIL_PATTERNS]
    n_idx = jnp.arange(S5_STATE, dtype=f32)
    return {
        "x_prompt": nrm((BATCH, SEQ, D_MODEL), 1.0),
        "x_sample": nrm((DEC_BATCH, DEC_SEQ, D_MODEL), 1.0),
        "cache_kv_w128": nrm((DEPTH, DEC_BATCH, lw[0], 2, HEADS_PER_GROUP, HEAD_DIM), 1.0),
        "cache_kv_w512": nrm((DEPTH, DEC_BATCH, lw[1], 2, HEADS_PER_GROUP, HEAD_DIM), 1.0),
        "cache_kv_w2048": nrm((DEPTH, DEC_BATCH, lw[2], 2, HEADS_PER_GROUP, HEAD_DIM), 1.0),
        "state_s5_re": nrm((DEPTH, DEC_BATCH, S5_GROUPS, S5_STATE), 0.1),
        "state_s5_im": nrm((DEPTH, DEC_BATCH, S5_GROUPS, S5_STATE), 0.1),
        "norm_mix": 1.0 + nrm((DEPTH, D_MODEL), 0.02),
        "w_in": nrm((DEPTH, D_MODEL, IN_W), D_MODEL ** -0.5),
        "q_norm": 1.0 + nrm((DEPTH, N_DIL_GROUPS, HEAD_DIM), 0.02),
        "k_norm": 1.0 + nrm((DEPTH, N_DIL_GROUPS, HEAD_DIM), 0.02),
        "s5_a_re": -0.5 + nrm((DEPTH, S5_GROUPS, S5_STATE), 0.01),
        "s5_a_im": math.pi * n_idx + nrm((DEPTH, S5_GROUPS, S5_STATE), 0.01),
        "s5_log_step": jax.random.uniform(next(ks), (DEPTH, S5_GROUPS), f32, math.log(1e-3), math.log(1e-1)),
        "s5_b_re": nrm((DEPTH, S5_GROUPS, S5_STATE, S5_GROUP_CH), (2 * S5_GROUP_CH) ** -0.5),
        "s5_b_im": nrm((DEPTH, S5_GROUPS, S5_STATE, S5_GROUP_CH), (2 * S5_GROUP_CH) ** -0.5),
        "s5_c_re": nrm((DEPTH, S5_GROUPS, S5_GROUP_CH, S5_STATE), (2 * S5_STATE) ** -0.5),
        "s5_c_im": nrm((DEPTH, S5_GROUPS, S5_GROUP_CH, S5_STATE), (2 * S5_STATE) ** -0.5),
        "s5_d": nrm((DEPTH, S5_GROUPS, S5_GROUP_CH), 1.0),
        "w_glu": nrm((DEPTH, S5_WIDTH, S5_WIDTH), S5_WIDTH ** -0.5),
        "b_glu": nrm((DEPTH, S5_WIDTH), 0.02),
        "w_branch_attn": nrm((DEPTH, ATTN_OUT_W, D_MODEL), ATTN_OUT_W ** -0.5),
        "w_branch_s5": nrm((DEPTH, S5_WIDTH, D_MODEL), S5_WIDTH ** -0.5),
        "w_out": nrm((DEPTH, D_MODEL, D_MODEL), D_MODEL ** -0.5),
        "norm_ffn": 1.0 + nrm((DEPTH, D_MODEL), 0.02),
        "w_router": nrm((DEPTH, D_MODEL, N_EXPERTS), D_MODEL ** -0.5),
        "b_router": nrm((DEPTH, N_EXPERTS), 0.01),
        "w_gate_up": nrm((DEPTH, N_EXPERTS, D_MODEL, 2 * D_EXPERT), D_MODEL ** -0.5),
        "b_gate_up": nrm((DEPTH, N_EXPERTS, 2 * D_EXPERT), 0.02),
        "w_down": nrm((DEPTH, N_EXPERTS, D_EXPERT, D_MODEL), D_EXPERT ** -0.5),
        "b_down": nrm((DEPTH, N_EXPERTS, D_MODEL), 0.02),
    }


def reference(x_prompt, x_sample, cache_kv_w128, cache_kv_w512, cache_kv_w2048, state_s5_re, state_s5_im,
              norm_mix, w_in, q_norm, k_norm, s5_a_re, s5_a_im, s5_log_step, s5_b_re, s5_b_im, s5_c_re,
              s5_c_im, s5_d, w_glu, b_glu, w_branch_attn, w_branch_s5, w_out, norm_ffn, w_router, b_router,
              w_gate_up, b_gate_up, w_down, b_down):
    pos_p = jnp.arange(x_prompt.shape[1], dtype=jnp.int32)
    pos_s = PAST_LEN + jnp.arange(x_sample.shape[1], dtype=jnp.int32)
    xp, xs = x_prompt, x_sample
    kv_p = [[], [], []]
    kv_s = [[], [], []]
    re_p, im_p, re_s, im_s = [], [], [], []
    for l in range(DEPTH):
        w = (norm_mix[l], w_in[l], q_norm[l], k_norm[l], s5_a_re[l], s5_a_im[l], s5_log_step[l],
             s5_b_re[l], s5_b_im[l], s5_c_re[l], s5_c_im[l], s5_d[l], w_glu[l], b_glu[l],
             w_branch_attn[l], w_branch_s5[l], w_out[l], norm_ffn[l], w_router[l], b_router[l],
             w_gate_up[l], b_gate_up[l], w_down[l], b_down[l])
        xp, nkv, hr, hi = hybrid_layer(xp, pos_p, None, None, *w)
        for g in range(N_DIL_GROUPS):
            kv_p[g].append(nkv[g])
        re_p.append(hr)
        im_p.append(hi)
        caches = (cache_kv_w128[l], cache_kv_w512[l], cache_kv_w2048[l])
        xs, nkv, hr, hi = hybrid_layer(xs, pos_s, caches, (state_s5_re[l], state_s5_im[l]), *w)
        for g in range(N_DIL_GROUPS):
            kv_s[g].append(nkv[g])
        re_s.append(hr)
        im_s.append(hi)
    return (xp, xs,
            jnp.stack(kv_p[0]), jnp.stack(kv_p[1]), jnp.stack(kv_p[2]), jnp.stack(re_p), jnp.stack(im_p),
            jnp.stack(kv_s[0]), jnp.stack(kv_s[1]), jnp.stack(kv_s[2]), jnp.stack(re_s), jnp.stack(im_s))
```

```python
import functools
import math

import jax
import jax.numpy as jnp
from jax import lax
from jax.experimental import pallas as pl
from jax.experimental.pallas import tpu as pltpu

F32 = jnp.float32
BF16 = jnp.bfloat16

HEAD_DIM = 128
HEADS = 4
DIL_PATTERNS = ((128, 1), (512, 4), (2048, 16))
NG = len(DIL_PATTERNS)
QBLOCK = 128
ROPE_THETA = 10000.0
PAST_LEN = 8192
S5_GROUP_CH = 16
S5_STATE = 64
TOP_K = 4
SWIGLU_LIMIT = 7.0
SWIGLU_ALPHA = 1.702
RMS_EPS = 1e-6

LANES = 128
SUBLANES = 8
ATTN_ROWS = max(d for _, d in DIL_PATTERNS) * QBLOCK
S5_CHUNK_GROUPS = 16
MOE_TILE = 256
VMEM_LIMIT = 56 * 1024 * 1024


def _cparams(sem, vmem=VMEM_LIMIT):
    return pltpu.CompilerParams(dimension_semantics=sem, vmem_limit_bytes=vmem)


def _dot(a, b):
    return jnp.dot(a, b, preferred_element_type=F32)


def _dot_nt(a, b):
    return lax.dot_general(a, b, (((1,), (1,)), ((), ())), preferred_element_type=F32)


def _const_spec(shape, single=True):
    nd = len(shape)
    kw = dict(pipeline_mode=pl.Buffered(1)) if single else {}
    return pl.BlockSpec(shape, lambda *_: (0,) * nd, **kw)


def _inproj_kernel(x_ref, nw_ref, w_ref, qn_ref, kn_ref, cos_ref, sin_ref, *outs, head_major, kb):
    if head_major:
        q_ref, k_ref, v_ref, kv0, kv1, kv2, u_ref, sa_ref, ss_ref = outs
    else:
        q_ref, kv0, kv1, kv2, u_ref, sa_ref, ss_ref = outs
    kvs = (kv0, kv1, kv2)
    x = x_ref[0]
    tm = x.shape[0]
    d = x.shape[1]
    h = x * lax.rsqrt(jnp.mean(x * x, axis=-1, keepdims=True) + RMS_EPS) * nw_ref[...]
    hb = h.astype(BF16)
    cos = cos_ref[...]
    sin = sin_ref[...]
    aw = NG * HEADS * HEAD_DIM
    gw = HEADS * HEAD_DIM

    def normrope(t, gvec):
        t = t * lax.rsqrt(jnp.mean(t * t, axis=-1, keepdims=True) + RMS_EPS) * gvec
        return t * cos + pltpu.roll(t, HEAD_DIM // 2, 1) * sin

    for g in range(NG):
        c0 = g * gw
        tq = _dot(hb, w_ref[:, c0:c0 + gw])
        tk = _dot(hb, w_ref[:, aw + c0:aw + c0 + gw])
        tv = _dot(hb, w_ref[:, 2 * aw + c0:2 * aw + c0 + gw])
        for hh in range(HEADS):
            sl = slice(hh * HEAD_DIM, (hh + 1) * HEAD_DIM)
            qh = normrope(tq[:, sl], qn_ref[g:g + 1, :])
            kh = normrope(tk[:, sl], kn_ref[g:g + 1, :])
            vh = tv[:, sl]
            if head_major:
                q_ref[0, g * HEADS + hh] = qh
                k_ref[0, g * HEADS + hh] = kh
                v_ref[0, g * HEADS + hh] = vh
            else:
                q_ref[0, :, c0 + hh * HEAD_DIM:c0 + (hh + 1) * HEAD_DIM] = qh
            kvs[g][0, :, hh * HEAD_DIM:(hh + 1) * HEAD_DIM] = kh[tm - kb[g]:]
            kvs[g][0, :, gw + hh * HEAD_DIM:gw + (hh + 1) * HEAD_DIM] = vh[tm - kb[g]:]
    base = 3 * aw
    u_ref[0] = _dot(hb, w_ref[:, base:base + d])
    sa_ref[0] = jax.nn.sigmoid(_dot(hb, w_ref[:, base + d:base + 2 * d]))
    ss_ref[0] = jax.nn.sigmoid(_dot(hb, w_ref[:, base + 2 * d:base + 3 * d]))


def _inproj(x, norm_w, w_bf, qn, kn, cos2, sin2, keeps, head_major, tm):
    b, s, d = x.shape
    nt = s // tm
    in_w = w_bf.shape[1]
    nh = NG * HEADS
    gw = HEADS * HEAD_DIM
    kb = tuple(min(k, tm) for k in keeps)
    in_specs = [
        pl.BlockSpec((1, tm, d), lambda bi, i: (bi, i, 0)),
        _const_spec((1, d)),
        _const_spec((d, in_w)),
        _const_spec((NG, HEAD_DIM)),
        _const_spec((NG, HEAD_DIM)),
        pl.BlockSpec((tm, HEAD_DIM), lambda bi, i: (i, 0)),
        pl.BlockSpec((tm, HEAD_DIM), lambda bi, i: (i, 0)),
    ]
    hm_spec = pl.BlockSpec((1, nh, tm, HEAD_DIM), lambda bi, i: (bi, 0, i, 0))
    row_spec = pl.BlockSpec((1, tm, d), lambda bi, i: (bi, i, 0))
    out_shape, out_specs = [], []
    if head_major:
        for _ in range(3):
            out_shape.append(jax.ShapeDtypeStruct((b, nh, s, HEAD_DIM), F32))
            out_specs.append(hm_spec)
    else:
        out_shape.append(jax.ShapeDtypeStruct((b, s, NG * gw), F32))
        out_specs.append(pl.BlockSpec((1, tm, NG * gw), lambda bi, i: (bi, i, 0)))
    for g in range(NG):
        nkeep = keeps[g] // kb[g]
        out_shape.append(jax.ShapeDtypeStruct((b, keeps[g], 2 * gw), F32))
        out_specs.append(pl.BlockSpec(
            (1, kb[g], 2 * gw), lambda bi, i, off=nt - nkeep: (bi, jnp.maximum(i - off, 0), 0)))
    for _ in range(3):
        out_shape.append(jax.ShapeDtypeStruct((b, s, d), F32))
        out_specs.append(row_spec)
    return pl.pallas_call(
        functools.partial(_inproj_kernel, head_major=head_major, kb=kb),
        grid=(b, nt), in_specs=in_specs, out_specs=out_specs, out_shape=out_shape,
        compiler_params=_cparams(("arbitrary", "arbitrary")),
        name="inproj_hm" if head_major else "inproj_rm",
    )(x, norm_w, w_bf, qn, kn, cos2, sin2)


def _attn_prompt_kernel(*refs):
    q_refs = refs[0:3]
    kc_refs = refs[3:6]
    vc_refs = refs[6:9]
    kp_refs = refs[9:12]
    vp_refs = refs[12:15]
    out_ref, o_scr, l_scr = refs[15:18]
    n = pl.program_id(2)
    neg_first = jnp.where(n > 0, 0.0, -jnp.inf).astype(F32)
    row = lax.broadcasted_iota(jnp.int32, (QBLOCK, QBLOCK), 0)
    col = lax.broadcasted_iota(jnp.int32, (QBLOCK, QBLOCK), 1)
    mask_p = col >= row
    mask_c = col <= row
    scale = HEAD_DIM ** -0.5

    def combo(q, kp, vp, kc, vc, negp):
        qb = q.astype(BF16)
        sp = _dot_nt(qb, kp.astype(BF16)) * scale
        sc = _dot_nt(qb, kc.astype(BF16)) * scale
        sp = jnp.where(mask_p, sp, -jnp.inf) + negp
        sc = jnp.where(mask_c, sc, -jnp.inf)
        m = jnp.maximum(jnp.max(sp, axis=-1, keepdims=True), jnp.max(sc, axis=-1, keepdims=True))
        pp = jnp.exp(sp - m)
        pc = jnp.exp(sc - m)
        den = jnp.sum(pp, axis=-1, keepdims=True) + jnp.sum(pc, axis=-1, keepdims=True)
        o = (_dot(pp.astype(BF16), vp.astype(BF16)) + _dot(pc.astype(BF16), vc.astype(BF16))) / den
        return o, m + jnp.log(den)

    for g, (window, dil) in enumerate(DIL_PATTERNS):
        assert window // dil == QBLOCK
        blk = dil * QBLOCK
        c = ATTN_ROWS // blk

        def rows(start, dil=dil):
            return pl.ds(start, QBLOCK) if dil == 1 else pl.ds(start, QBLOCK, stride=dil)

        def emit(start, o, lse, g=g, rows=rows):
            o_scr[g, rows(start), :] = o
            l_scr[g, rows(start), :] = jnp.broadcast_to(lse, (QBLOCK, HEAD_DIM))

        def first_body(r, carry, g=g, rows=rows, emit=emit):
            o, lse = combo(q_refs[g][0, 0, rows(r), :], kp_refs[g][0, 0, rows(r), :],
                           vp_refs[g][0, 0, rows(r), :], kc_refs[g][0, 0, rows(r), :],
                           vc_refs[g][0, 0, rows(r), :], neg_first)
            emit(r, o, lse)
            return carry

        lax.fori_loop(0, dil, first_body, 0)

        def rest_body(idx, carry, g=g, rows=rows, emit=emit, dil=dil, blk=blk):
            start = (idx // dil + 1) * blk + idx % dil
            o, lse = combo(q_refs[g][0, 0, rows(start), :], kc_refs[g][0, 0, rows(start - blk), :],
                           vc_refs[g][0, 0, rows(start - blk), :], kc_refs[g][0, 0, rows(start), :],
                           vc_refs[g][0, 0, rows(start), :], 0.0)
            emit(start, o, lse)
            return carry

        if c > 1:
            lax.fori_loop(0, (c - 1) * dil, rest_body, 0)

    def mix_body(i, carry):
        rs = pl.ds(pl.multiple_of(i * QBLOCK, QBLOCK), QBLOCK)
        l0, l1, l2 = l_scr[0, rs, :], l_scr[1, rs, :], l_scr[2, rs, :]
        m = jnp.maximum(jnp.maximum(l0, l1), l2)
        e0, e1, e2 = jnp.exp(l0 - m), jnp.exp(l1 - m), jnp.exp(l2 - m)
        num = e0 * o_scr[0, rs, :] + e1 * o_scr[1, rs, :] + e2 * o_scr[2, rs, :]
        out_ref[0, 0, rs, :] = num / (e0 + e1 + e2)
        return carry

    lax.fori_loop(0, ATTN_ROWS // QBLOCK, mix_body, 0)


def _attn_prompt(q12, k12, v12):
    b, _, s, _ = q12.shape
    assert s % ATTN_ROWS == 0
    nb = s // ATTN_ROWS

    def cur_spec(g):
        return pl.BlockSpec((1, 1, ATTN_ROWS, HEAD_DIM), lambda bi, h, n, g=g: (bi, g * HEADS + h, n, 0))

    def prev_spec(g):
        blk = DIL_PATTERNS[g][1] * QBLOCK
        c = ATTN_ROWS // blk
        return pl.BlockSpec((1, 1, blk, HEAD_DIM),
                            lambda bi, h, n, g=g, c=c: (bi, g * HEADS + h, jnp.maximum(n * c - 1, 0), 0))

    in_specs = ([cur_spec(g) for g in range(NG)] * 3 + [prev_spec(g) for g in range(NG)] * 2)
    args = [q12] * 3 + [k12] * 3 + [v12] * 3 + [k12] * 3 + [v12] * 3
    return pl.pallas_call(
        _attn_prompt_kernel,
        grid=(b, HEADS, nb), in_specs=in_specs,
        out_specs=pl.BlockSpec((1, 1, ATTN_ROWS, HEAD_DIM), lambda bi, h, n: (bi, h, n, 0)),
        out_shape=jax.ShapeDtypeStruct((b, HEADS, s, HEAD_DIM), F32),
        scratch_shapes=[pltpu.VMEM((NG, ATTN_ROWS, HEAD_DIM), F32), pltpu.VMEM((NG, ATTN_ROWS, HEAD_DIM), F32)],
        compiler_params=_cparams(("arbitrary", "arbitrary", "arbitrary")),
        name="attn_prompt",
    )(*args)


def _attn_decode_kernel(q_ref, n0, n1, n2, c0, c1, c2, o_ref):
    scale = HEAD_DIM ** -0.5
    outs, lses = [], []
    for g, (n_ref, c_ref) in enumerate(((n0, c0), (n1, c1), (n2, c2))):
        q = q_ref[:, g]
        kn, vn = n_ref[:, 0], n_ref[:, 1]
        kc, vc = c_ref[:, :, 0], c_ref[:, :, 1]
        s_c = jnp.sum(kc * q[:, None], axis=-1, keepdims=True) * scale
        s_n = jnp.sum(kn * q, axis=-1, keepdims=True) * scale
        m = jnp.maximum(jnp.max(s_c, axis=1), s_n)
        p_c = jnp.exp(s_c - m[:, None])
        p_n = jnp.exp(s_n - m)
        den = jnp.sum(p_c, axis=1) + p_n
        outs.append((jnp.sum(p_c * vc, axis=1) + p_n * vn) / den)
        lses.append(m + jnp.log(den))
    m = jnp.maximum(jnp.maximum(lses[0], lses[1]), lses[2])
    es = [jnp.exp(l - m) for l in lses]
    o_ref[...] = (es[0] * outs[0] + es[1] * outs[1] + es[2] * outs[2]) / (es[0] + es[1] + es[2])


def _attn_decode(q4, kv_new, caches, bt=4):
    db = q4.shape[0]
    in_specs = [pl.BlockSpec((bt, NG, HEADS, HEAD_DIM), lambda i: (i, 0, 0, 0))]
    in_specs += [pl.BlockSpec((bt, 2, HEADS, HEAD_DIM), lambda i: (i, 0, 0, 0))] * NG
    cargs = []
    for g, (window, dil) in enumerate(DIL_PATTERNS):
        span = window // dil
        cache = caches[g]
        assert cache.shape[1] == span * dil, "decode path supports a full window buffer only"
        cargs.append(cache.reshape(db, span, dil, 2, HEADS, HEAD_DIM))
        in_specs.append(pl.BlockSpec((bt, span, None, 2, HEADS, HEAD_DIM), lambda i: (i, 0, 0, 0, 0, 0)))
    return pl.pallas_call(
        _attn_decode_kernel,
        grid=(db // bt,), in_specs=in_specs,
        out_specs=pl.BlockSpec((bt, HEADS, HEAD_DIM), lambda i: (i, 0, 0)),
        out_shape=jax.ShapeDtypeStruct((db, HEADS, HEAD_DIM), F32),
        compiler_params=_cparams(("arbitrary",)),
        name="attn_decode",
    )(q4, *kv_new, *cargs)


def _gelu(y):
    return 0.5 * y * (1.0 + jnp.tanh(math.sqrt(2.0 / math.pi) * (y + 0.044715 * (y * y * y))))


def _s5_prompt_kernel(u_ref, wbr_ref, wbi_ref, wcr_ref, wci_ref, ar_ref, ai_ref, d_ref,
                      z_ref, hr_ref, hi_ref, sre, sim, st_re, st_im):
    t = pl.program_id(1)
    nb, tc, _ = u_ref.shape
    nj = sre.shape[1] // tc

    @pl.when(t == 0)
    def _():
        st_re[...] = jnp.zeros_like(st_re)
        st_im[...] = jnp.zeros_like(st_im)

    for b in range(nb):
        ub = u_ref[b].astype(BF16)
        bre = _dot(ub, wbr_ref[0])
        bim = _dot(ub, wbi_ref[0])
        for j in range(nj):
            sre[b, pl.ds(j, tc, stride=nj), :] = bre[:, j * LANES:(j + 1) * LANES]
            sim[b, pl.ds(j, tc, stride=nj), :] = bim[:, j * LANES:(j + 1) * LANES]

    ar = ar_ref[0]
    ai = ai_ref[0]

    def step(i, carry):
        rs = pl.ds(pl.multiple_of(i * nj, nj), nj)
        new = []
        for b in range(nb):
            hr, hi = carry[2 * b], carry[2 * b + 1]
            nr = ar * hr - ai * hi + sre[b, rs, :]
            ni = ar * hi + ai * hr + sim[b, rs, :]
            sre[b, rs, :] = nr
            sim[b, rs, :] = ni
            new += [nr, ni]
        return tuple(new)

    init = []
    for b in range(nb):
        init += [st_re[b], st_im[b]]
    fin = lax.fori_loop(0, tc, step, tuple(init), unroll=2)
    for b in range(nb):
        st_re[b] = fin[2 * b]
        st_im[b] = fin[2 * b + 1]
        hr_ref[b, 0] = fin[2 * b]
        hi_ref[b, 0] = fin[2 * b + 1]

    for b in range(nb):
        hre = jnp.concatenate([sre[b, pl.ds(j, tc, stride=nj), :] for j in range(nj)], axis=1).astype(BF16)
        him = jnp.concatenate([sim[b, pl.ds(j, tc, stride=nj), :] for j in range(nj)], axis=1).astype(BF16)
        y = _dot(hre, wcr_ref[0]) + _dot(him, wci_ref[0]) + d_ref[0] * u_ref[b]
        z_ref[b] = _gelu(y)


def _s5_prompt(u, wbr, wbi, wcr, wci, ar, ai, dsk, tc=512):
    b, s, w = u.shape
    nc, cw, fw = wbr.shape
    nj = fw // LANES
    assert nj == SUBLANES and s % tc == 0
    wspec = lambda shp: pl.BlockSpec((1,) + shp, lambda c, t: (c, 0, 0))
    st_shape = jax.ShapeDtypeStruct((b, nc, nj, LANES), F32)
    st_spec = pl.BlockSpec((b, 1, nj, LANES), lambda c, t: (0, c, 0, 0))
    return pl.pallas_call(
        _s5_prompt_kernel,
        grid=(nc, s // tc),
        in_specs=[pl.BlockSpec((b, tc, cw), lambda c, t: (0, t, c)),
                  wspec((cw, fw)), wspec((cw, fw)), wspec((fw, cw)), wspec((fw, cw)),
                  wspec((nj, LANES)), wspec((nj, LANES)), wspec((1, cw))],
        out_specs=[pl.BlockSpec((b, tc, cw), lambda c, t: (0, t, c)), st_spec, st_spec],
        out_shape=[jax.ShapeDtypeStruct((b, s, w), F32), st_shape, st_shape],
        scratch_shapes=[pltpu.VMEM((b, tc * nj, LANES), F32), pltpu.VMEM((b, tc * nj, LANES), F32),
                        pltpu.VMEM((b, nj, LANES), F32), pltpu.VMEM((b, nj, LANES), F32)],
        compiler_params=_cparams(("arbitrary", "arbitrary")),
        name="s5_prompt",
    )(u, wbr, wbi, wcr, wci, ar, ai, dsk)


def _s5_step_kernel(u_ref, h0r_ref, h0i_ref, wbr_ref, wbi_ref, wcr_ref, wci_ref, ar_ref, ai_ref, d_ref,
                    z_ref, hr_ref, hi_ref):
    nc, cw, fw = wbr_ref.shape
    for c in range(nc):
        cs = slice(c * cw, (c + 1) * cw)
        fs = slice(c * fw, (c + 1) * fw)
        uc = u_ref[:, cs]
        ub = uc.astype(BF16)
        ar, ai = ar_ref[:, fs], ai_ref[:, fs]
        h0r, h0i = h0r_ref[:, fs], h0i_ref[:, fs]
        hr = ar * h0r - ai * h0i + _dot(ub, wbr_ref[c])
        hi = ar * h0i + ai * h0r + _dot(ub, wbi_ref[c])
        hr_ref[:, fs] = hr
        hi_ref[:, fs] = hi
        y = _dot(hr.astype(BF16), wcr_ref[c]) + _dot(hi.astype(BF16), wci_ref[c]) + d_ref[:, cs] * uc
        z_ref[:, cs] = _gelu(y)


def _s5_step(u, h0r, h0i, wbr, wbi, wcr, wci, ar, ai, dsk):
    db, w = u.shape
    nf = h0r.shape[1]
    shapes = [jax.ShapeDtypeStruct((db, w), F32), jax.ShapeDtypeStruct((db, nf), F32),
              jax.ShapeDtypeStruct((db, nf), F32)]
    return pl.pallas_call(
        _s5_step_kernel, out_shape=shapes, compiler_params=_cparams(None), name="s5_step",
    )(u, h0r, h0i, wbr, wbi, wcr, wci, ar, ai, dsk)


def _s5_params(a_re, a_im, log_step, b_re, b_im, c_re, c_im, d_skip):
    g, n = a_re.shape
    ch = b_re.shape[2]
    cg = S5_CHUNK_GROUPS
    nc = g // cg
    step = jnp.exp(log_step)[:, None]
    mag = jnp.exp(a_re * step)
    abr = mag * jnp.cos(a_im * step)
    abi = mag * jnp.sin(a_im * step)
    nr, ni = abr - 1.0, abi
    den = a_re * a_re + a_im * a_im
    qr = (nr * a_re + ni * a_im) / den
    qi = (ni * a_re - nr * a_im) / den
    bbr = qr[..., None] * b_re - qi[..., None] * b_im
    bbi = qr[..., None] * b_im + qi[..., None] * b_re
    eye = jnp.eye(cg, dtype=F32)

    def in_mat(bb):
        t = bb.reshape(nc, cg, n, ch).transpose(0, 1, 3, 2)
        return jnp.einsum("kgcn,gh->kgchn", t, eye).reshape(nc, cg * ch, cg * n).astype(BF16)

    def out_mat(cc):
        t = cc.reshape(nc, cg, ch, n).transpose(0, 1, 3, 2)
        return jnp.einsum("kgnc,gh->kgnhc", t, eye).reshape(nc, cg * n, cg * ch).astype(BF16)

    return (in_mat(bbr), in_mat(bbi), out_mat(c_re), out_mat(-c_im),
            abr.reshape(nc, cg * n), abi.reshape(nc, cg * n), d_skip.reshape(nc, 1, cg * ch))


def _merge_kernel(x_ref, at_ref, z_ref, sa_ref, ss_ref, wglu_ref, bglu_ref, wba_ref, wbs_ref, wout_ref,
                  nf_ref, wr_ref, br_ref, base_ref,
                  x1_ref, h2_ref, eidx_ref, gate_ref, rank_ref, cnt_ref, cnt_scr):
    first = jnp.logical_and(pl.program_id(0) == 0, pl.program_id(1) == 0)

    @pl.when(first)
    def _():
        cnt_scr[...] = base_ref[...]

    x = x_ref[0]
    tm = x.shape[0]
    attn = jnp.concatenate([at_ref[0, hh] for hh in range(HEADS)], axis=1).astype(BF16)
    z = z_ref[0]
    zb = z.astype(BF16)
    s5o = z * jax.nn.sigmoid(_dot(zb, wglu_ref[...]) + bglu_ref[...])
    merged = sa_ref[0] * _dot(attn, wba_ref[...]) + ss_ref[0] * _dot(s5o.astype(BF16), wbs_ref[...])
    x1 = x + _dot(merged.astype(BF16), wout_ref[...])
    x1_ref[0] = x1
    h2 = x1 * lax.rsqrt(jnp.mean(x1 * x1, axis=-1, keepdims=True) + RMS_EPS) * nf_ref[...]
    for j in range(h2.shape[1] // LANES):
        h2_ref[pl.ds(j, tm, stride=SUBLANES), :] = h2[:, j * LANES:(j + 1) * LANES]

    logits = jnp.dot(h2, wr_ref[...], precision=lax.Precision.HIGHEST, preferred_element_type=F32) + br_ref[...]
    ne = logits.shape[1]
    lane = lax.broadcasted_iota(jnp.int32, (tm, ne), 1)
    work = logits
    vals, idxs, sels = [], [], []
    for _ in range(TOP_K):
        m = jnp.max(work, axis=-1, keepdims=True)
        idx = jnp.min(jnp.where(work == m, lane, ne), axis=-1, keepdims=True)
        sel = lane == idx
        vals.append(m)
        idxs.append(idx)
        sels.append(sel)
        work = jnp.where(sel, -jnp.inf, work)
    es = [jnp.exp(v - vals[0]) for v in vals]
    den = es[0] + es[1] + es[2] + es[3]
    member = jnp.zeros((tm, ne), F32)
    for sel in sels:
        member = member + sel.astype(F32)
    r = lax.broadcasted_iota(jnp.int32, (tm, tm), 0)
    c = lax.broadcasted_iota(jnp.int32, (tm, tm), 1)
    ltri = (c < r).astype(BF16)
    tot = cnt_scr[...] + _dot(ltri, member.astype(BF16))
    k4 = lax.broadcasted_iota(jnp.int32, (tm, TOP_K), 1)
    eidx = jnp.zeros((tm, TOP_K), jnp.int32)
    gate = jnp.zeros((tm, TOP_K), F32)
    rank = jnp.zeros((tm, TOP_K), jnp.int32)
    for k in range(TOP_K):
        rk = jnp.sum(jnp.where(sels[k], tot, 0.0), axis=-1, keepdims=True).astype(jnp.int32)
        eidx = jnp.where(k4 == k, idxs[k], eidx)
        gate = jnp.where(k4 == k, es[k] / den, gate)
        rank = jnp.where(k4 == k, rk, rank)
    eidx_ref[0] = eidx
    gate_ref[0] = gate
    rank_ref[0] = rank
    cnt_scr[...] = cnt_scr[...] + jnp.sum(member, axis=0, keepdims=True)
    cnt_ref[...] = cnt_scr[...]


def _merge(x, attn_hm, z, sa, ss, wglu, bglu, wba, wbs, wout, nf, wr, br, base_cnt, tm):
    b, s, d = x.shape
    nt = s // tm
    ne = wr.shape[1]
    row = pl.BlockSpec((1, tm, d), lambda bi, i: (bi, i, 0))
    k_spec = pl.BlockSpec((1, tm, TOP_K), lambda bi, i: (bi, i, 0))
    in_specs = [row, pl.BlockSpec((1, HEADS, tm, HEAD_DIM), lambda bi, i: (bi, 0, i, 0)), row, row, row,
                _const_spec(wglu.shape), _const_spec(bglu.shape), _const_spec(wba.shape), _const_spec(wbs.shape),
                _const_spec(wout.shape), _const_spec(nf.shape), _const_spec(wr.shape), _const_spec(br.shape),
                _const_spec(base_cnt.shape)]
    out_shape = [jax.ShapeDtypeStruct((b, s, d), F32),
                 jax.ShapeDtypeStruct((b * s * SUBLANES, LANES), F32),
                 jax.ShapeDtypeStruct((b, s, TOP_K), jnp.int32),
                 jax.ShapeDtypeStruct((b, s, TOP_K), F32),
                 jax.ShapeDtypeStruct((b, s, TOP_K), jnp.int32),
                 jax.ShapeDtypeStruct((1, ne), F32)]
    out_specs = [row, pl.BlockSpec((tm * SUBLANES, LANES), lambda bi, i: (bi * nt + i, 0)),
                 k_spec, k_spec, k_spec, pl.BlockSpec((1, ne), lambda bi, i: (0, 0))]
    return pl.pallas_call(
        _merge_kernel, grid=(b, nt), in_specs=in_specs, out_specs=out_specs, out_shape=out_shape,
        scratch_shapes=[pltpu.VMEM((1, ne), F32)],
        compiler_params=_cparams(("arbitrary", "arbitrary")),
        name="merge_router",
    )(x, attn_hm, z, sa, ss, wglu, bglu, wba, wbs, wout, nf, wr, br, base_cnt)


def _expert_kernel(blk_e_ref, nused_ref, src_ref, h2_hbm, wgu_ref, bgu_ref, wd_ref, bd_ref, y_hbm,
                   xbuf, ybuf, wgu_bf, wd_bf, gsem, ssem, *, n_tok):
    b = pl.program_id(0)
    nused = nused_ref[0]
    tmx = ybuf.shape[0] // SUBLANES
    d = wgu_bf.shape[0]

    def gather_copy(tok, slot, i):
        return pltpu.make_async_copy(h2_hbm.at[pl.ds(tok * SUBLANES, SUBLANES), :],
                                     xbuf.at[slot, pl.ds(i * SUBLANES, SUBLANES), :], gsem.at[slot])

    def scatter_copy(pos, i):
        return pltpu.make_async_copy(ybuf.at[pl.ds(i * SUBLANES, SUBLANES), :],
                                     y_hbm.at[pl.ds(pos * SUBLANES, SUBLANES), :], ssem.at[0])

    def issue_gather(blk, slot):
        def body(i, carry):
            s = src_ref[blk * tmx + i]
            tok = jnp.where(s >= 0, s >> 2, 0)
            gather_copy(tok, slot, i).start()
            return carry
        lax.fori_loop(0, tmx, body, 0)

    @pl.when(b < nused)
    def _():
        slot = b % 2

        @pl.when(b == 0)
        def _():
            issue_gather(0, 0)

        @pl.when(b + 1 < nused)
        def _():
            issue_gather(b + 1, 1 - slot)

        def wait_g(i, carry):
            gather_copy(0, slot, i).wait()
            return carry
        lax.fori_loop(0, tmx, wait_g, 0)

        new_expert = jnp.logical_or(b == 0, blk_e_ref[b] != blk_e_ref[jnp.maximum(b - 1, 0)])

        @pl.when(new_expert)
        def _():
            wgu_bf[...] = wgu_ref[0].astype(BF16)
            wd_bf[...] = wd_ref[0].astype(BF16)

        x = jnp.concatenate([xbuf[slot, pl.ds(j, tmx, stride=SUBLANES), :] for j in range(d // LANES)],
                            axis=1).astype(BF16)
        gu = _dot(x, wgu_bf[...]) + bgu_ref[0]
        de = gu.shape[1] // 2
        gl = jnp.minimum(gu[:, :de], SWIGLU_LIMIT)
        lin = jnp.clip(gu[:, de:], -SWIGLU_LIMIT, SWIGLU_LIMIT)
        act = gl * jax.nn.sigmoid(SWIGLU_ALPHA * gl) * (lin + 1.0)
        y = _dot(act.astype(BF16), wd_bf[...]) + bd_ref[0]
        for j in range(d // LANES):
            ybuf[pl.ds(j, tmx, stride=SUBLANES), :] = y[:, j * LANES:(j + 1) * LANES]

        def issue_s(i, carry):
            s = src_ref[b * tmx + i]

            @pl.when(s >= 0)
            def _():
                scatter_copy((s & 3) * n_tok + (s >> 2), i).start()
            return carry
        lax.fori_loop(0, tmx, issue_s, 0)

        def wait_s(i, carry):
            s = src_ref[b * tmx + i]

            @pl.when(s >= 0)
            def _():
                scatter_copy(0, i).wait()
            return carry
        lax.fori_loop(0, tmx, wait_s, 0)


def _experts(blk_e, nused, src, h2, wgu, bgu, wd, bd, n_tok):
    ne, d, de2 = wgu.shape
    nb = blk_e.shape[0]
    tmx = MOE_TILE
    grid_spec = pltpu.PrefetchScalarGridSpec(
        num_scalar_prefetch=3, grid=(nb,),
        in_specs=[pl.BlockSpec(memory_space=pl.ANY),
                  pl.BlockSpec((1, d, de2), lambda b, e, n, s: (e[b], 0, 0)),
                  pl.BlockSpec((1, 1, de2), lambda b, e, n, s: (e[b], 0, 0)),
                  pl.BlockSpec((1, de2 // 2, d), lambda b, e, n, s: (e[b], 0, 0)),
                  pl.BlockSpec((1, 1, d), lambda b, e, n, s: (e[b], 0, 0))],
        out_specs=pl.BlockSpec(memory_space=pl.ANY),
        scratch_shapes=[pltpu.VMEM((2, tmx * SUBLANES, LANES), F32), pltpu.VMEM((tmx * SUBLANES, LANES), F32),
                        pltpu.VMEM((d, de2), BF16), pltpu.VMEM((de2 // 2, d), BF16),
                        pltpu.SemaphoreType.DMA((2,)), pltpu.SemaphoreType.DMA((1,))])
    return pl.pallas_call(
        functools.partial(_expert_kernel, n_tok=n_tok), grid_spec=grid_spec,
        out_shape=jax.ShapeDtypeStruct((TOP_K * n_tok * SUBLANES, LANES), F32),
        compiler_params=_cparams(("arbitrary",)),
        name="experts",
    )(blk_e, nused, src, h2, wgu, bgu.reshape(ne, 1, de2), wd, bd.reshape(ne, 1, d))


def _combine_kernel(x1_ref, gate_ref, y_ref, o_ref):
    tc = x1_ref.shape[1]
    gate = gate_ref[0]
    for j in range(x1_ref.shape[2] // LANES):
        acc = x1_ref[0, :, j * LANES:(j + 1) * LANES]
        for k in range(TOP_K):
            acc = acc + gate[:, k:k + 1] * y_ref[k, pl.ds(j, tc, stride=SUBLANES), :]
        o_ref[0, :, j * LANES:(j + 1) * LANES] = acc


def _combine(x1, gate, y4, tok_off, tc):
    b, s, d = x1.shape
    nt = s // tc
    off = tok_off // tc
    return pl.pallas_call(
        _combine_kernel, grid=(b, nt),
        in_specs=[pl.BlockSpec((1, tc, d), lambda bi, i: (bi, i, 0)),
                  pl.BlockSpec((1, tc, TOP_K), lambda bi, i: (bi, i, 0)),
                  pl.BlockSpec((TOP_K, tc * SUBLANES, LANES), lambda bi, i: (0, off + bi * nt + i, 0))],
        out_specs=pl.BlockSpec((1, tc, d), lambda bi, i: (bi, i, 0)),
        out_shape=jax.ShapeDtypeStruct((b, s, d), F32),
        compiler_params=_cparams(("arbitrary", "arbitrary")),
        name="combine",
    )(x1, gate, y4)


def _rope_tables(pos):
    half = HEAD_DIM // 2
    freq = ROPE_THETA ** (-jnp.arange(half, dtype=F32) / half)
    ang = pos.astype(F32)[:, None] * freq[None, :]
    cos, sin = jnp.cos(ang), jnp.sin(ang)
    return jnp.concatenate([cos, cos], axis=-1), jnp.concatenate([-sin, sin], axis=-1)


def _layer(xp, xs, caches, h0r, h0i, norm_mix, w_in, q_norm, k_norm, s5_a_re, s5_a_im, s5_log_step,
           s5_b_re, s5_b_im, s5_c_re, s5_c_im, s5_d, w_glu, b_glu, w_branch_attn, w_branch_s5, w_out,
           norm_ffn, w_router, b_router, w_gate_up, b_gate_up, w_down, b_down):
    bp, sp, d = xp.shape
    db, ds, _ = xs.shape
    assert ds == 1, "decode path handles one new token per sequence"
    ne = w_router.shape[1]
    gw = HEADS * HEAD_DIM
    w_in_bf = w_in.astype(BF16)
    nm = norm_mix.reshape(1, d)
    s5p = _s5_params(s5_a_re, s5_a_im, s5_log_step, s5_b_re, s5_b_im, s5_c_re, s5_c_im, s5_d)
    wbr, wbi, wcr, wci, abr, abi, dsk = s5p
    nc = wbr.shape[0]

    cos_p, sin_p = _rope_tables(jnp.arange(sp, dtype=jnp.int32))
    keeps_p = tuple(min(w, sp) for w, _ in DIL_PATTERNS)
    q12, k12, v12, kv0, kv1, kv2, u_p, sa_p, ss_p = _inproj(
        xp, nm, w_in_bf, q_norm, k_norm, cos_p, sin_p, keeps_p, True, 256)
    attn_p = _attn_prompt(q12, k12, v12)
    z_p, hr_p, hi_p = _s5_prompt(u_p, wbr, wbi, wcr, wci, abr.reshape(nc, SUBLANES, LANES),
                                 abi.reshape(nc, SUBLANES, LANES), dsk)

    cos_s, sin_s = _rope_tables(jnp.full((db,), PAST_LEN, jnp.int32))
    q_s, kvs0, kvs1, kvs2, u_s, sa_s, ss_s = _inproj(
        xs.reshape(1, db, d), nm, w_in_bf, q_norm, k_norm, cos_s, sin_s, (db,) * NG, False, db)
    kv_new = [kv.reshape(db, 2, HEADS, HEAD_DIM) for kv in (kvs0, kvs1, kvs2)]
    attn_s = _attn_decode(q_s.reshape(db, NG, HEADS, HEAD_DIM), kv_new, caches)
    attn_s = attn_s.transpose(1, 0, 2).reshape(1, HEADS, db, HEAD_DIM)
    z_s, hr_s, hi_s = _s5_step(u_s.reshape(db, d), h0r.reshape(db, -1), h0i.reshape(db, -1), wbr, wbi, wcr, wci,
                               abr.reshape(1, -1), abi.reshape(1, -1), dsk.reshape(1, -1))

    mw = (w_glu.astype(BF16), b_glu.reshape(1, d), w_branch_attn.astype(BF16), w_branch_s5.astype(BF16),
          w_out.astype(BF16), norm_ffn.reshape(1, d), w_router, b_router.reshape(1, ne))
    x1_p, h2_p, e_p, g_p, r_p, cnt_p = _merge(xp, attn_p, z_p, sa_p, ss_p, *mw, jnp.zeros((1, ne), F32), 256)
    x1_s, h2_s, e_s, g_s, r_s, cnt = _merge(xs.reshape(1, db, d), attn_s, z_s.reshape(1, db, d), sa_s, ss_s,
                                            *mw, cnt_p, db)

    n_tok = bp * sp + db
    eidx = jnp.concatenate([e_p.reshape(-1, TOP_K), e_s.reshape(-1, TOP_K)], axis=0)
    rank = jnp.concatenate([r_p.reshape(-1, TOP_K), r_s.reshape(-1, TOP_K)], axis=0)
    counts = cnt.reshape(ne).astype(jnp.int32)
    pcounts = (counts + MOE_TILE - 1) // MOE_TILE * MOE_TILE
    pend = jnp.cumsum(pcounts)
    pstart = pend - pcounts
    dest = jnp.sum(jnp.where(eidx[..., None] == jnp.arange(ne, dtype=jnp.int32), pstart, 0), axis=-1) + rank
    nb = -(-(n_tok * TOP_K) // MOE_TILE) + ne
    src = jnp.full((nb * MOE_TILE,), -1, jnp.int32).at[dest.reshape(-1)].set(
        jnp.arange(n_tok * TOP_K, dtype=jnp.int32))
    nused = (pend[-1] // MOE_TILE).astype(jnp.int32)
    blk_start = jnp.minimum(jnp.arange(nb, dtype=jnp.int32), nused - 1) * MOE_TILE
    blk_e = jnp.sum(blk_start[:, None] >= pend[None, :], axis=-1).astype(jnp.int32)
    h2 = jnp.concatenate([h2_p, h2_s], axis=0)
    y4 = _experts(blk_e, nused.reshape(1), src, h2, w_gate_up, b_gate_up, w_down, b_down, n_tok)
    y4 = y4.reshape(TOP_K, n_tok * SUBLANES, LANES)
    y_p = _combine(x1_p, g_p, y4, 0, 256)
    y_s = _combine(x1_s, g_s, y4, bp * sp, db)

    def kv_leaf(kv, n):
        return kv.reshape(kv.shape[0], n, 2, HEADS, HEAD_DIM)

    g_n = s5_a_re.shape
    outs = dict(
        y_p=y_p, y_s=y_s.reshape(db, 1, d),
        kv_p=[kv_leaf(kv, keeps_p[g]) for g, kv in enumerate((kv0, kv1, kv2))],
        kv_s=[kv.reshape(db, 1, 2, HEADS, HEAD_DIM) for kv in (kvs0, kvs1, kvs2)],
        re_p=hr_p.reshape(bp, *g_n), im_p=hi_p.reshape(bp, *g_n),
        re_s=hr_s.reshape(db, *g_n), im_s=hi_s.reshape(db, *g_n))
    return outs


def kernel(x_prompt, x_sample, cache_kv_w128, cache_kv_w512, cache_kv_w2048, state_s5_re, state_s5_im, norm_mix, w_in, q_norm, k_norm, s5_a_re, s5_a_im, s5_log_step, s5_b_re, s5_b_im, s5_c_re, s5_c_im, s5_d, w_glu, b_glu, w_branch_attn, w_branch_s5, w_out, norm_ffn, w_router, b_router, w_gate_up, b_gate_up, w_down, b_down):
    depth = norm_mix.shape[0]
    xp, xs = x_prompt, x_sample
    acc = {k: [] for k in ("re_p", "im_p", "re_s", "im_s")}
    kv_p = [[] for _ in range(NG)]
    kv_s = [[] for _ in range(NG)]
    for l in range(depth):
        caches = (cache_kv_w128[l], cache_kv_w512[l], cache_kv_w2048[l])
        o = _layer(xp, xs, caches, state_s5_re[l], state_s5_im[l], norm_mix[l], w_in[l], q_norm[l], k_norm[l],
                   s5_a_re[l], s5_a_im[l], s5_log_step[l], s5_b_re[l], s5_b_im[l], s5_c_re[l], s5_c_im[l],
                   s5_d[l], w_glu[l], b_glu[l], w_branch_attn[l], w_branch_s5[l], w_out[l], norm_ffn[l],
                   w_router[l], b_router[l], w_gate_up[l], b_gate_up[l], w_down[l], b_down[l])
        xp, xs = o["y_p"], o["y_s"]
        for g in range(NG):
            kv_p[g].append(o["kv_p"][g])
            kv_s[g].append(o["kv_s"][g])
        for k in acc:
            acc[k].append(o[k])
    st = lambda xs_: jnp.stack(xs_)
    return (xp, xs, st(kv_p[0]), st(kv_p[1]), st(kv_p[2]), st(acc["re_p"]), st(acc["im_p"]),
            st(kv_s[0]), st(kv_s[1]), st(kv_s[2]), st(acc["re_s"]), st(acc["im_s"]))
```

```python
import functools
import math

import jax
import jax.numpy as jnp
from jax import lax
from jax.experimental import pallas as pl
from jax.experimental.pallas import tpu as pltpu

F32 = jnp.float32
BF16 = jnp.bfloat16

HEAD_DIM = 128
HEADS = 4
DIL_PATTERNS = ((128, 1), (512, 4), (2048, 16))
NG = len(DIL_PATTERNS)
QBLOCK = 128
ROPE_THETA = 10000.0
PAST_LEN = 8192
S5_GROUP_CH = 16
S5_STATE = 64
TOP_K = 4
SWIGLU_LIMIT = 7.0
SWIGLU_ALPHA = 1.702
RMS_EPS = 1e-6

LANES = 128
SUBLANES = 8
ATTN_ROWS = max(d for _, d in DIL_PATTERNS) * QBLOCK
S5_CHUNK_GROUPS = 16
MOE_TILE = 256
VMEM_LIMIT = 56 * 1024 * 1024


def _cparams(sem, vmem=VMEM_LIMIT):
    return pltpu.CompilerParams(dimension_semantics=sem, vmem_limit_bytes=vmem)


def _dot(a, b):
    return jnp.dot(a, b, preferred_element_type=F32)


def _dot_nt(a, b):
    return lax.dot_general(a, b, (((1,), (1,)), ((), ())), preferred_element_type=F32)


def _const_spec(shape, single=True):
    nd = len(shape)
    kw = dict(pipeline_mode=pl.Buffered(1)) if single else {}
    return pl.BlockSpec(shape, lambda *_: (0,) * nd, **kw)


def _inproj_kernel(x_ref, nw_ref, w_ref, qn_ref, kn_ref, cos_ref, sin_ref, *outs, head_major, kb):
    if head_major:
        q_ref, k_ref, v_ref, kv0, kv1, kv2, u_ref, sa_ref, ss_ref = outs
    else:
        q_ref, kv0, kv1, kv2, u_ref, sa_ref, ss_ref = outs
    kvs = (kv0, kv1, kv2)
    x = x_ref[0]
    tm = x.shape[0]
    d = x.shape[1]
    h = x * lax.rsqrt(jnp.mean(x * x, axis=-1, keepdims=True) + RMS_EPS) * nw_ref[...]
    hb = h.astype(BF16)
    cos = cos_ref[...]
    sin = sin_ref[...]
    aw = NG * HEADS * HEAD_DIM
    gw = HEADS * HEAD_DIM

    def normrope(t, gvec):
        t = t * lax.rsqrt(jnp.mean(t * t, axis=-1, keepdims=True) + RMS_EPS) * gvec
        return t * cos + pltpu.roll(t, HEAD_DIM // 2, 1) * sin

    for g in range(NG):
        c0 = g * gw
        tq = _dot(hb, w_ref[:, c0:c0 + gw])
        tk = _dot(hb, w_ref[:, aw + c0:aw + c0 + gw])
        tv = _dot(hb, w_ref[:, 2 * aw + c0:2 * aw + c0 + gw])
        for hh in range(HEADS):
            sl = slice(hh * HEAD_DIM, (hh + 1) * HEAD_DIM)
            qh = normrope(tq[:, sl], qn_ref[g:g + 1, :])
            kh = normrope(tk[:, sl], kn_ref[g:g + 1, :])
            vh = tv[:, sl]
            if head_major:
                q_ref[0, g * HEADS + hh] = qh
                k_ref[0, g * HEADS + hh] = kh
                v_ref[0, g * HEADS + hh] = vh
            else:
                q_ref[0, :, c0 + hh * HEAD_DIM:c0 + (hh + 1) * HEAD_DIM] = qh
            kvs[g][0, :, 0, hh, :] = kh[tm - kb[g]:]
            kvs[g][0, :, 1, hh, :] = vh[tm - kb[g]:]
    base = 3 * aw
    u_ref[0] = _dot(hb, w_ref[:, base:base + d])
    sa_ref[0] = jax.nn.sigmoid(_dot(hb, w_ref[:, base + d:base + 2 * d]))
    ss_ref[0] = jax.nn.sigmoid(_dot(hb, w_ref[:, base + 2 * d:base + 3 * d]))


def _inproj(x, norm_w, w_bf, qn, kn, cos2, sin2, keeps, head_major, tm):
    b, s, d = x.shape
    nt = s // tm
    in_w = w_bf.shape[1]
    nh = NG * HEADS
    gw = HEADS * HEAD_DIM
    kb = tuple(min(k, tm) for k in keeps)
    in_specs = [
        pl.BlockSpec((1, tm, d), lambda bi, i: (bi, i, 0)),
        _const_spec((1, d)),
        _const_spec((d, in_w)),
        _const_spec((NG, HEAD_DIM)),
        _const_spec((NG, HEAD_DIM)),
        pl.BlockSpec((tm, HEAD_DIM), lambda bi, i: (i, 0)),
        pl.BlockSpec((tm, HEAD_DIM), lambda bi, i: (i, 0)),
    ]
    hm_spec = pl.BlockSpec((1, nh, tm, HEAD_DIM), lambda bi, i: (bi, 0, i, 0))
    row_spec = pl.BlockSpec((1, tm, d), lambda bi, i: (bi, i, 0))
    out_shape, out_specs = [], []
    if head_major:
        for _ in range(3):
            out_shape.append(jax.ShapeDtypeStruct((b, nh, s, HEAD_DIM), F32))
            out_specs.append(hm_spec)
    else:
        out_shape.append(jax.ShapeDtypeStruct((b, s, NG * gw), F32))
        out_specs.append(pl.BlockSpec((1, tm, NG * gw), lambda bi, i: (bi, i, 0)))
    for g in range(NG):
        nkeep = keeps[g] // kb[g]
        out_shape.append(jax.ShapeDtypeStruct((b, keeps[g], 2, HEADS, HEAD_DIM), F32))
        out_specs.append(pl.BlockSpec(
            (1, kb[g], 2, HEADS, HEAD_DIM),
            lambda bi, i, off=nt - nkeep: (bi, jnp.maximum(i - off, 0), 0, 0, 0)))
    for _ in range(3):
        out_shape.append(jax.ShapeDtypeStruct((b, s, d), F32))
        out_specs.append(row_spec)
    return pl.pallas_call(
        functools.partial(_inproj_kernel, head_major=head_major, kb=kb),
        grid=(b, nt), in_specs=in_specs, out_specs=out_specs, out_shape=out_shape,
        compiler_params=_cparams(("arbitrary", "arbitrary")),
        name="inproj_hm" if head_major else "inproj_rm",
    )(x, norm_w, w_bf, qn, kn, cos2, sin2)


def _attn_prompt_kernel(*refs):
    q_refs = refs[0:3]
    kc_refs = refs[3:6]
    vc_refs = refs[6:9]
    kp_refs = refs[9:12]
    vp_refs = refs[12:15]
    out_ref, o_scr, l_scr = refs[15:18]
    n = pl.program_id(2)
    neg_first = jnp.where(n > 0, 0.0, -jnp.inf).astype(F32)
    row = lax.broadcasted_iota(jnp.int32, (QBLOCK, QBLOCK), 0)
    col = lax.broadcasted_iota(jnp.int32, (QBLOCK, QBLOCK), 1)
    mask_p = col >= row
    mask_c = col <= row
    scale = HEAD_DIM ** -0.5

    def combo(q, kp, vp, kc, vc, negp):
        qb = q.astype(BF16)
        sp = _dot_nt(qb, kp.astype(BF16)) * scale
        sc = _dot_nt(qb, kc.astype(BF16)) * scale
        sp = jnp.where(mask_p, sp, -jnp.inf) + negp
        sc = jnp.where(mask_c, sc, -jnp.inf)
        m = jnp.maximum(jnp.max(sp, axis=-1, keepdims=True), jnp.max(sc, axis=-1, keepdims=True))
        pp = jnp.exp(sp - m)
        pc = jnp.exp(sc - m)
        den = jnp.sum(pp, axis=-1, keepdims=True) + jnp.sum(pc, axis=-1, keepdims=True)
        o = (_dot(pp.astype(BF16), vp.astype(BF16)) + _dot(pc.astype(BF16), vc.astype(BF16))) / den
        return o, m + jnp.log(den)

    for g, (window, dil) in enumerate(DIL_PATTERNS):
        assert window // dil == QBLOCK
        blk = dil * QBLOCK
        c = ATTN_ROWS // blk

        def rows(start, dil=dil):
            return pl.ds(start, QBLOCK) if dil == 1 else pl.ds(start, QBLOCK, stride=dil)

        def emit(start, o, lse, g=g, rows=rows):
            o_scr[g, rows(start), :] = o
            l_scr[g, rows(start), :] = jnp.broadcast_to(lse, (QBLOCK, HEAD_DIM))

        def first_body(r, carry, g=g, rows=rows, emit=emit):
            o, lse = combo(q_refs[g][0, 0, rows(r), :], kp_refs[g][0, 0, rows(r), :],
                           vp_refs[g][0, 0, rows(r), :], kc_refs[g][0, 0, rows(r), :],
                           vc_refs[g][0, 0, rows(r), :], neg_first)
            emit(r, o, lse)
            return carry

        lax.fori_loop(0, dil, first_body, 0)

        def rest_body(idx, carry, g=g, rows=rows, emit=emit, dil=dil, blk=blk):
            start = (idx // dil + 1) * blk + idx % dil
            o, lse = combo(q_refs[g][0, 0, rows(start), :], kc_refs[g][0, 0, rows(start - blk), :],
                           vc_refs[g][0, 0, rows(start - blk), :], kc_refs[g][0, 0, rows(start), :],
                           vc_refs[g][0, 0, rows(start), :], 0.0)
            emit(start, o, lse)
            return carry

        if c > 1:
            lax.fori_loop(0, (c - 1) * dil, rest_body, 0)

    def mix_body(i, carry):
        rs = pl.ds(pl.multiple_of(i * QBLOCK, QBLOCK), QBLOCK)
        l0, l1, l2 = l_scr[0, rs, :], l_scr[1, rs, :], l_scr[2, rs, :]
        m = jnp.maximum(jnp.maximum(l0, l1), l2)
        e0, e1, e2 = jnp.exp(l0 - m), jnp.exp(l1 - m), jnp.exp(l2 - m)
        num = e0 * o_scr[0, rs, :] + e1 * o_scr[1, rs, :] + e2 * o_scr[2, rs, :]
        out_ref[0, 0, rs, :] = num / (e0 + e1 + e2)
        return carry

    lax.fori_loop(0, ATTN_ROWS // QBLOCK, mix_body, 0)


def _attn_prompt(q12, k12, v12):
    b, _, s, _ = q12.shape
    assert s % ATTN_ROWS == 0
    nb = s // ATTN_ROWS

    def cur_spec(g):
        return pl.BlockSpec((1, 1, ATTN_ROWS, HEAD_DIM), lambda bi, h, n, g=g: (bi, g * HEADS + h, n, 0))

    def prev_spec(g):
        blk = DIL_PATTERNS[g][1] * QBLOCK
        c = ATTN_ROWS // blk
        return pl.BlockSpec((1, 1, blk, HEAD_DIM),
                            lambda bi, h, n, g=g, c=c: (bi, g * HEADS + h, jnp.maximum(n * c - 1, 0), 0))

    in_specs = ([cur_spec(g) for g in range(NG)] * 3 + [prev_spec(g) for g in range(NG)] * 2)
    args = [q12] * 3 + [k12] * 3 + [v12] * 3 + [k12] * 3 + [v12] * 3
    return pl.pallas_call(
        _attn_prompt_kernel,
        grid=(b, HEADS, nb), in_specs=in_specs,
        out_specs=pl.BlockSpec((1, 1, ATTN_ROWS, HEAD_DIM), lambda bi, h, n: (bi, h, n, 0)),
        out_shape=jax.ShapeDtypeStruct((b, HEADS, s, HEAD_DIM), F32),
        scratch_shapes=[pltpu.VMEM((NG, ATTN_ROWS, HEAD_DIM), F32), pltpu.VMEM((NG, ATTN_ROWS, HEAD_DIM), F32)],
        compiler_params=_cparams(("arbitrary", "arbitrary", "arbitrary")),
        name="attn_prompt",
    )(*args)


def _attn_decode_kernel(q_ref, n0, n1, n2, c0, c1, c2, o_ref):
    scale = HEAD_DIM ** -0.5
    outs, lses = [], []
    for g, (n_ref, c_ref) in enumerate(((n0, c0), (n1, c1), (n2, c2))):
        q = q_ref[:, g]
        kn, vn = n_ref[:, 0], n_ref[:, 1]
        kc, vc = c_ref[:, :, 0], c_ref[:, :, 1]
        s_c = jnp.sum(kc * q[:, None], axis=-1, keepdims=True) * scale
        s_n = jnp.sum(kn * q, axis=-1, keepdims=True) * scale
        m = jnp.maximum(jnp.max(s_c, axis=1), s_n)
        p_c = jnp.exp(s_c - m[:, None])
        p_n = jnp.exp(s_n - m)
        den = jnp.sum(p_c, axis=1) + p_n
        outs.append((jnp.sum(p_c * vc, axis=1) + p_n * vn) / den)
        lses.append(m + jnp.log(den))
    m = jnp.maximum(jnp.maximum(lses[0], lses[1]), lses[2])
    es = [jnp.exp(l - m) for l in lses]
    o_ref[...] = (es[0] * outs[0] + es[1] * outs[1] + es[2] * outs[2]) / (es[0] + es[1] + es[2])


def _attn_decode(q4, kv_new, caches, bt=4):
    db = q4.shape[0]
    in_specs = [pl.BlockSpec((bt, NG, HEADS, HEAD_DIM), lambda i: (i, 0, 0, 0))]
    in_specs += [pl.BlockSpec((bt, 2, HEADS, HEAD_DIM), lambda i: (i, 0, 0, 0))] * NG
    cargs = []
    for g, (window, dil) in enumerate(DIL_PATTERNS):
        span = window // dil
        cache = caches[g]
        assert cache.shape[1] == span * dil, "decode path supports a full window buffer only"
        cargs.append(cache.reshape(db, span, dil, 2, HEADS, HEAD_DIM))
        in_specs.append(pl.BlockSpec((bt, span, None, 2, HEADS, HEAD_DIM), lambda i: (i, 0, 0, 0, 0, 0)))
    return pl.pallas_call(
        _attn_decode_kernel,
        grid=(db // bt,), in_specs=in_specs,
        out_specs=pl.BlockSpec((bt, HEADS, HEAD_DIM), lambda i: (i, 0, 0)),
        out_shape=jax.ShapeDtypeStruct((db, HEADS, HEAD_DIM), F32),
        compiler_params=_cparams(("arbitrary",)),
        name="attn_decode",
    )(q4, *kv_new, *cargs)


def _gelu(y):
    return 0.5 * y * (1.0 + jnp.tanh(math.sqrt(2.0 / math.pi) * (y + 0.044715 * (y * y * y))))


def _s5_prompt_kernel(u_ref, wbr_ref, wbi_ref, wcr_ref, wci_ref, ar_ref, ai_ref, d_ref,
                      z_ref, hr_ref, hi_ref, sre, sim, st_re, st_im):
    t = pl.program_id(1)
    nb, tc, _ = u_ref.shape
    nj = sre.shape[1] // tc

    @pl.when(t == 0)
    def _():
        st_re[...] = jnp.zeros_like(st_re)
        st_im[...] = jnp.zeros_like(st_im)

    for b in range(nb):
        ub = u_ref[b].astype(BF16)
        bre = _dot(ub, wbr_ref[0])
        bim = _dot(ub, wbi_ref[0])
        for j in range(nj):
            sre[b, pl.ds(j, tc, stride=nj), :] = bre[:, j * LANES:(j + 1) * LANES]
            sim[b, pl.ds(j, tc, stride=nj), :] = bim[:, j * LANES:(j + 1) * LANES]

    ar = ar_ref[0]
    ai = ai_ref[0]

    def step(i, carry):
        rs = pl.ds(pl.multiple_of(i * nj, nj), nj)
        new = []
        for b in range(nb):
            hr, hi = carry[2 * b], carry[2 * b + 1]
            nr = ar * hr - ai * hi + sre[b, rs, :]
            ni = ar * hi + ai * hr + sim[b, rs, :]
            sre[b, rs, :] = nr
            sim[b, rs, :] = ni
            new += [nr, ni]
        return tuple(new)

    init = []
    for b in range(nb):
        init += [st_re[b], st_im[b]]
    fin = lax.fori_loop(0, tc, step, tuple(init), unroll=2)
    for b in range(nb):
        st_re[b] = fin[2 * b]
        st_im[b] = fin[2 * b + 1]
        hr_ref[b, 0] = fin[2 * b]
        hi_ref[b, 0] = fin[2 * b + 1]

    for b in range(nb):
        hre = jnp.concatenate([sre[b, pl.ds(j, tc, stride=nj), :] for j in range(nj)], axis=1).astype(BF16)
        him = jnp.concatenate([sim[b, pl.ds(j, tc, stride=nj), :] for j in range(nj)], axis=1).astype(BF16)
        y = _dot(hre, wcr_ref[0]) + _dot(him, wci_ref[0]) + d_ref[0] * u_ref[b]
        z_ref[b] = _gelu(y)


def _s5_prompt(u, wbr, wbi, wcr, wci, ar, ai, dsk, tc=512):
    b, s, w = u.shape
    nc, cw, fw = wbr.shape
    nj = fw // LANES
    assert nj == SUBLANES and s % tc == 0
    wspec = lambda shp: pl.BlockSpec((1,) + shp, lambda c, t: (c, 0, 0))
    st_shape = jax.ShapeDtypeStruct((b, nc, nj, LANES), F32)
    st_spec = pl.BlockSpec((b, 1, nj, LANES), lambda c, t: (0, c, 0, 0))
    return pl.pallas_call(
        _s5_prompt_kernel,
        grid=(nc, s // tc),
        in_specs=[pl.BlockSpec((b, tc, cw), lambda c, t: (0, t, c)),
                  wspec((cw, fw)), wspec((cw, fw)), wspec((fw, cw)), wspec((fw, cw)),
                  wspec((nj, LANES)), wspec((nj, LANES)), wspec((1, cw))],
        out_specs=[pl.BlockSpec((b, tc, cw), lambda c, t: (0, t, c)), st_spec, st_spec],
        out_shape=[jax.ShapeDtypeStruct((b, s, w), F32), st_shape, st_shape],
        scratch_shapes=[pltpu.VMEM((b, tc * nj, LANES), F32), pltpu.VMEM((b, tc * nj, LANES), F32),
                        pltpu.VMEM((b, nj, LANES), F32), pltpu.VMEM((b, nj, LANES), F32)],
        compiler_params=_cparams(("arbitrary", "arbitrary")),
        name="s5_prompt",
    )(u, wbr, wbi, wcr, wci, ar, ai, dsk)


def _s5_step_kernel(u_ref, h0r_ref, h0i_ref, wbr_ref, wbi_ref, wcr_ref, wci_ref, ar_ref, ai_ref, d_ref,
                    z_ref, hr_ref, hi_ref):
    nc, cw, fw = wbr_ref.shape
    for c in range(nc):
        cs = slice(c * cw, (c + 1) * cw)
        fs = slice(c * fw, (c + 1) * fw)
        uc = u_ref[:, cs]
        ub = uc.astype(BF16)
        ar, ai = ar_ref[:, fs], ai_ref[:, fs]
        h0r, h0i = h0r_ref[:, fs], h0i_ref[:, fs]
        hr = ar * h0r - ai * h0i + _dot(ub, wbr_ref[c])
        hi = ar * h0i + ai * h0r + _dot(ub, wbi_ref[c])
        hr_ref[:, fs] = hr
        hi_ref[:, fs] = hi
        y = _dot(hr.astype(BF16), wcr_ref[c]) + _dot(hi.astype(BF16), wci_ref[c]) + d_ref[:, cs] * uc
        z_ref[:, cs] = _gelu(y)


def _s5_step(u, h0r, h0i, wbr, wbi, wcr, wci, ar, ai, dsk):
    db, w = u.shape
    nf = h0r.shape[1]
    shapes = [jax.ShapeDtypeStruct((db, w), F32), jax.ShapeDtypeStruct((db, nf), F32),
              jax.ShapeDtypeStruct((db, nf), F32)]
    return pl.pallas_call(
        _s5_step_kernel, out_shape=shapes, compiler_params=_cparams(None), name="s5_step",
    )(u, h0r, h0i, wbr, wbi, wcr, wci, ar, ai, dsk)


def _s5_params(a_re, a_im, log_step, b_re, b_im, c_re, c_im, d_skip):
    g, n = a_re.shape
    ch = b_re.shape[2]
    cg = S5_CHUNK_GROUPS
    nc = g // cg
    step = jnp.exp(log_step)[:, None]
    mag = jnp.exp(a_re * step)
    abr = mag * jnp.cos(a_im * step)
    abi = mag * jnp.sin(a_im * step)
    nr, ni = abr - 1.0, abi
    den = a_re * a_re + a_im * a_im
    qr = (nr * a_re + ni * a_im) / den
    qi = (ni * a_re - nr * a_im) / den
    bbr = qr[..., None] * b_re - qi[..., None] * b_im
    bbi = qr[..., None] * b_im + qi[..., None] * b_re
    eye = jnp.eye(cg, dtype=F32)

    def in_mat(bb):
        t = bb.reshape(nc, cg, n, ch).transpose(0, 1, 3, 2)
        return jnp.einsum("kgcn,gh->kgchn", t, eye).reshape(nc, cg * ch, cg * n).astype(BF16)

    def out_mat(cc):
        t = cc.reshape(nc, cg, ch, n).transpose(0, 1, 3, 2)
        return jnp.einsum("kgnc,gh->kgnhc", t, eye).reshape(nc, cg * n, cg * ch).astype(BF16)

    return (in_mat(bbr), in_mat(bbi), out_mat(c_re), out_mat(-c_im),
            abr.reshape(nc, cg * n), abi.reshape(nc, cg * n), d_skip.reshape(nc, 1, cg * ch))


def _merge_kernel(x_ref, at_ref, z_ref, sa_ref, ss_ref, wglu_ref, bglu_ref, wba_ref, wbs_ref, wout_ref,
                  nf_ref, wr_ref, br_ref, base_ref,
                  x1_ref, h2_ref, eidx_ref, gate_ref, rank_ref, cnt_ref, cnt_scr):
    first = jnp.logical_and(pl.program_id(0) == 0, pl.program_id(1) == 0)

    @pl.when(first)
    def _():
        cnt_scr[...] = base_ref[...]

    x = x_ref[0]
    tm = x.shape[0]
    attn = jnp.concatenate([at_ref[0, hh] for hh in range(HEADS)], axis=1).astype(BF16)
    z = z_ref[0]
    zb = z.astype(BF16)
    s5o = z * jax.nn.sigmoid(_dot(zb, wglu_ref[...]) + bglu_ref[...])
    merged = sa_ref[0] * _dot(attn, wba_ref[...]) + ss_ref[0] * _dot(s5o.astype(BF16), wbs_ref[...])
    x1 = x + _dot(merged.astype(BF16), wout_ref[...])
    x1_ref[0] = x1
    h2 = x1 * lax.rsqrt(jnp.mean(x1 * x1, axis=-1, keepdims=True) + RMS_EPS) * nf_ref[...]
    for j in range(h2.shape[1] // LANES):
        h2_ref[pl.ds(j, tm, stride=SUBLANES), :] = h2[:, j * LANES:(j + 1) * LANES]

    logits = _dot(h2.astype(BF16), wr_ref[...]) + br_ref[...]
    ne = logits.shape[1]
    lane = lax.broadcasted_iota(jnp.int32, (tm, ne), 1)
    work = logits
    vals, idxs, sels = [], [], []
    for _ in range(TOP_K):
        m = jnp.max(work, axis=-1, keepdims=True)
        idx = jnp.min(jnp.where(work == m, lane, ne), axis=-1, keepdims=True)
        sel = lane == idx
        vals.append(m)
        idxs.append(idx)
        sels.append(sel)
        work = jnp.where(sel, -jnp.inf, work)
    es = [jnp.exp(v - vals[0]) for v in vals]
    den = es[0] + es[1] + es[2] + es[3]
    member = jnp.zeros((tm, ne), F32)
    for sel in sels:
        member = member + sel.astype(F32)
    r = lax.broadcasted_iota(jnp.int32, (tm, tm), 0)
    c = lax.broadcasted_iota(jnp.int32, (tm, tm), 1)
    ltri = (c < r).astype(BF16)
    tot = cnt_scr[...] + _dot(ltri, member.astype(BF16))
    k4 = lax.broadcasted_iota(jnp.int32, (tm, TOP_K), 1)
    eidx = jnp.zeros((tm, TOP_K), jnp.int32)
    gate = jnp.zeros((tm, TOP_K), F32)
    rank = jnp.zeros((tm, TOP_K), jnp.int32)
    for k in range(TOP_K):
        rk = jnp.sum(jnp.where(sels[k], tot, 0.0), axis=-1, keepdims=True).astype(jnp.int32)
        eidx = jnp.where(k4 == k, idxs[k], eidx)
        gate = jnp.where(k4 == k, es[k] / den, gate)
        rank = jnp.where(k4 == k, rk, rank)
    eidx_ref[0] = eidx
    gate_ref[0] = gate
    rank_ref[0] = rank
    cnt_scr[...] = cnt_scr[...] + jnp.sum(member, axis=0, keepdims=True)
    cnt_ref[...] = cnt_scr[...]


def _merge(x, attn_hm, z, sa, ss, wglu, bglu, wba, wbs, wout, nf, wr, br, base_cnt, tm):
    b, s, d = x.shape
    nt = s // tm
    ne = wr.shape[1]
    row = pl.BlockSpec((1, tm, d), lambda bi, i: (bi, i, 0))
    k_spec = pl.BlockSpec((1, tm, TOP_K), lambda bi, i: (bi, i, 0))
    in_specs = [row, pl.BlockSpec((1, HEADS, tm, HEAD_DIM), lambda bi, i: (bi, 0, i, 0)), row, row, row,
                _const_spec(wglu.shape), _const_spec(bglu.shape), _const_spec(wba.shape), _const_spec(wbs.shape),
                _const_spec(wout.shape), _const_spec(nf.shape), _const_spec(wr.shape), _const_spec(br.shape),
                _const_spec(base_cnt.shape)]
    out_shape = [jax.ShapeDtypeStruct((b, s, d), F32),
                 jax.ShapeDtypeStruct((b * s * SUBLANES, LANES), F32),
                 jax.ShapeDtypeStruct((b, s, TOP_K), jnp.int32),
                 jax.ShapeDtypeStruct((b, s, TOP_K), F32),
                 jax.ShapeDtypeStruct((b, s, TOP_K), jnp.int32),
                 jax.ShapeDtypeStruct((1, ne), F32)]
    out_specs = [row, pl.BlockSpec((tm * SUBLANES, LANES), lambda bi, i: (bi * nt + i, 0)),
                 k_spec, k_spec, k_spec, pl.BlockSpec((1, ne), lambda bi, i: (0, 0))]
    return pl.pallas_call(
        _merge_kernel, grid=(b, nt), in_specs=in_specs, out_specs=out_specs, out_shape=out_shape,
        scratch_shapes=[pltpu.VMEM((1, ne), F32)],
        compiler_params=_cparams(("arbitrary", "arbitrary")),
        name="merge_router",
    )(x, attn_hm, z, sa, ss, wglu, bglu, wba, wbs, wout, nf, wr, br, base_cnt)


def _expert_kernel(blk_e_ref, nused_ref, nvalid_ref, dest_ref, h2_hbm, wgu_ref, bgu_ref, wd_ref, bd_ref, y_hbm,
                   src, xbuf, ybuf, wgu_bf, wd_bf, gsem, ssem, *, n_tok):
    b = pl.program_id(0)
    nused = nused_ref[0]
    tmx = ybuf.shape[1] // SUBLANES
    d = wgu_bf.shape[0]
    unroll = SUBLANES

    def issue_gather(blk, slot):
        last = nvalid_ref[blk] - 1

        def body(i, carry):
            s = src[blk * tmx + jnp.minimum(i, last)]
            pltpu.make_async_copy(h2_hbm.at[pl.ds((s >> 2) * SUBLANES, SUBLANES), :],
                                  xbuf.at[slot, pl.ds(i * SUBLANES, SUBLANES), :], gsem.at[slot]).start()
            return carry
        lax.fori_loop(0, tmx, body, 0, unroll=unroll)

    def wait_scatter(slot, n):
        rows = pl.ds(0, n * SUBLANES)
        pltpu.make_async_copy(ybuf.at[slot, rows, :], y_hbm.at[rows, :], ssem.at[0]).wait()

    @pl.when(b < nused)
    def _():
        slot = b % 2

        @pl.when(b == 0)
        def _():
            def inv(a, carry):
                src[dest_ref[a]] = a
                return carry
            lax.fori_loop(0, n_tok * TOP_K, inv, 0, unroll=unroll)
            issue_gather(0, 0)

        @pl.when(b + 1 < nused)
        def _():
            issue_gather(b + 1, 1 - slot)

        pltpu.make_async_copy(h2_hbm.at[pl.ds(0, tmx * SUBLANES), :], xbuf.at[slot], gsem.at[slot]).wait()

        new_expert = jnp.logical_or(b == 0, blk_e_ref[b] != blk_e_ref[jnp.maximum(b - 1, 0)])

        @pl.when(new_expert)
        def _():
            wgu_bf[...] = wgu_ref[0].astype(BF16)
            wd_bf[...] = wd_ref[0].astype(BF16)

        x = jnp.concatenate([xbuf[slot, pl.ds(j, tmx, stride=SUBLANES), :] for j in range(d // LANES)],
                            axis=1).astype(BF16)
        gu = _dot(x, wgu_bf[...]) + bgu_ref[0]
        de = gu.shape[1] // 2
        gl = jnp.minimum(gu[:, :de], SWIGLU_LIMIT)
        lin = jnp.clip(gu[:, de:], -SWIGLU_LIMIT, SWIGLU_LIMIT)
        act = gl * jax.nn.sigmoid(SWIGLU_ALPHA * gl) * (lin + 1.0)
        y = _dot(act.astype(BF16), wd_bf[...]) + bd_ref[0]
        for j in range(d // LANES):
            ybuf[slot, pl.ds(j, tmx, stride=SUBLANES), :] = y[:, j * LANES:(j + 1) * LANES]

        @pl.when(b > 0)
        def _():
            wait_scatter(1 - slot, nvalid_ref[jnp.maximum(b - 1, 0)])

        nv = nvalid_ref[b]

        def issue_s(i):
            s = src[b * tmx + i]
            pos = (s & (TOP_K - 1)) * n_tok + (s >> 2)
            pltpu.make_async_copy(ybuf.at[slot, pl.ds(i * SUBLANES, SUBLANES), :],
                                  y_hbm.at[pl.ds(pos * SUBLANES, SUBLANES), :], ssem.at[0]).start()

        def issue_group(c, carry):
            for u in range(unroll):
                issue_s(c * unroll + u)
            return carry

        def issue_one(i, carry):
            issue_s(i)
            return carry

        nfull = nv // unroll
        lax.fori_loop(0, nfull, issue_group, 0)
        lax.fori_loop(nfull * unroll, nv, issue_one, 0)

        @pl.when(b == nused - 1)
        def _():
            wait_scatter(slot, nv)


def _experts(blk_e, nused, nvalid, dest, h2, wgu, bgu, wd, bd):
    assert TOP_K == 4
    ne, d, de2 = wgu.shape
    nb = blk_e.shape[0]
    n_tok = dest.shape[0] // TOP_K
    tmx = MOE_TILE
    wmap = lambda b, e, *_: (e[b], 0, 0)
    grid_spec = pltpu.PrefetchScalarGridSpec(
        num_scalar_prefetch=4, grid=(nb,),
        in_specs=[pl.BlockSpec(memory_space=pl.ANY),
                  pl.BlockSpec((1, d, de2), wmap), pl.BlockSpec((1, 1, de2), wmap),
                  pl.BlockSpec((1, de2 // 2, d), wmap), pl.BlockSpec((1, 1, d), wmap)],
        out_specs=pl.BlockSpec(memory_space=pl.ANY),
        scratch_shapes=[pltpu.SMEM((nb * tmx,), jnp.int32),
                        pltpu.VMEM((2, tmx * SUBLANES, LANES), F32), pltpu.VMEM((2, tmx * SUBLANES, LANES), F32),
                        pltpu.VMEM((d, de2), BF16), pltpu.VMEM((de2 // 2, d), BF16),
                        pltpu.SemaphoreType.DMA((2,)), pltpu.SemaphoreType.DMA((1,))])
    return pl.pallas_call(
        functools.partial(_expert_kernel, n_tok=n_tok), grid_spec=grid_spec,
        out_shape=jax.ShapeDtypeStruct((TOP_K * n_tok * SUBLANES, LANES), F32),
        compiler_params=_cparams(("arbitrary",)),
        name="experts",
    )(blk_e, nused, nvalid, dest, h2, wgu, bgu.reshape(ne, 1, de2), wd, bd.reshape(ne, 1, d))


def _combine_kernel(x1_ref, gate_ref, *refs):
    y_refs, o_ref = refs[:TOP_K], refs[TOP_K]
    tc = x1_ref.shape[1]
    gate = gate_ref[0]
    for j in range(x1_ref.shape[2] // LANES):
        acc = x1_ref[0, :, j * LANES:(j + 1) * LANES]
        for k in range(TOP_K):
            acc = acc + gate[:, k:k + 1] * y_refs[k][pl.ds(j, tc, stride=SUBLANES), :]
        o_ref[0, :, j * LANES:(j + 1) * LANES] = acc


def _combine(x1, gate, y, n_slot, tok_off, tc):
    b, s, d = x1.shape
    nt = s // tc
    assert n_slot % tc == 0 and tok_off % tc == 0
    y_specs = [pl.BlockSpec((tc * SUBLANES, LANES),
                            lambda bi, i, base=(k * n_slot + tok_off) // tc: (base + bi * nt + i, 0))
               for k in range(TOP_K)]
    return pl.pallas_call(
        _combine_kernel, grid=(b, nt),
        in_specs=[pl.BlockSpec((1, tc, d), lambda bi, i: (bi, i, 0)),
                  pl.BlockSpec((1, tc, TOP_K), lambda bi, i: (bi, i, 0))] + y_specs,
        out_specs=pl.BlockSpec((1, tc, d), lambda bi, i: (bi, i, 0)),
        out_shape=jax.ShapeDtypeStruct((b, s, d), F32),
        compiler_params=_cparams(("arbitrary", "arbitrary")),
        name="combine",
    )(x1, gate, *([y] * TOP_K))


def _rope_tables(pos):
    half = HEAD_DIM // 2
    freq = ROPE_THETA ** (-jnp.arange(half, dtype=F32) / half)
    ang = pos.astype(F32)[:, None] * freq[None, :]
    cos, sin = jnp.cos(ang), jnp.sin(ang)
    return jnp.concatenate([cos, cos], axis=-1), jnp.concatenate([-sin, sin], axis=-1)


def _layer(xp, xs, caches, h0r, h0i, norm_mix, w_in, q_norm, k_norm, s5_a_re, s5_a_im, s5_log_step,
           s5_b_re, s5_b_im, s5_c_re, s5_c_im, s5_d, w_glu, b_glu, w_branch_attn, w_branch_s5, w_out,
           norm_ffn, w_router, b_router, w_gate_up, b_gate_up, w_down, b_down):
    bp, sp, d = xp.shape
    db, ds, _ = xs.shape
    assert ds == 1, "decode path handles one new token per sequence"
    ne = w_router.shape[1]
    gw = HEADS * HEAD_DIM
    w_in_bf = w_in.astype(BF16)
    nm = norm_mix.reshape(1, d)
    s5p = _s5_params(s5_a_re, s5_a_im, s5_log_step, s5_b_re, s5_b_im, s5_c_re, s5_c_im, s5_d)
    wbr, wbi, wcr, wci, abr, abi, dsk = s5p
    nc = wbr.shape[0]

    cos_p, sin_p = _rope_tables(jnp.arange(sp, dtype=jnp.int32))
    keeps_p = tuple(min(w, sp) for w, _ in DIL_PATTERNS)
    q12, k12, v12, kv0, kv1, kv2, u_p, sa_p, ss_p = _inproj(
        xp, nm, w_in_bf, q_norm, k_norm, cos_p, sin_p, keeps_p, True, 256)
    attn_p = _attn_prompt(q12, k12, v12)
    z_p, hr_p, hi_p = _s5_prompt(u_p, wbr, wbi, wcr, wci, abr.reshape(nc, SUBLANES, LANES),
                                 abi.reshape(nc, SUBLANES, LANES), dsk)

    cos_s, sin_s = _rope_tables(jnp.full((db,), PAST_LEN, jnp.int32))
    q_s, kvs0, kvs1, kvs2, u_s, sa_s, ss_s = _inproj(
        xs.reshape(1, db, d), nm, w_in_bf, q_norm, k_norm, cos_s, sin_s, (db,) * NG, False, db)
    kv_new = [kv.reshape(db, 2, HEADS, HEAD_DIM) for kv in (kvs0, kvs1, kvs2)]
    attn_s = _attn_decode(q_s.reshape(db, NG, HEADS, HEAD_DIM), kv_new, caches)
    attn_s = attn_s.transpose(1, 0, 2).reshape(1, HEADS, db, HEAD_DIM)
    z_s, hr_s, hi_s = _s5_step(u_s.reshape(db, d), h0r.reshape(db, -1), h0i.reshape(db, -1), wbr, wbi, wcr, wci,
                               abr.reshape(1, -1), abi.reshape(1, -1), dsk.reshape(1, -1))

    mw = (w_glu.astype(BF16), b_glu.reshape(1, d), w_branch_attn.astype(BF16), w_branch_s5.astype(BF16),
          w_out.astype(BF16), norm_ffn.reshape(1, d), w_router.astype(BF16), b_router.reshape(1, ne))
    x1_p, h2_p, e_p, g_p, r_p, cnt_p = _merge(xp, attn_p, z_p, sa_p, ss_p, *mw, jnp.zeros((1, ne), F32), 256)
    x1_s, h2_s, e_s, g_s, r_s, cnt = _merge(xs.reshape(1, db, d), attn_s, z_s.reshape(1, db, d), sa_s, ss_s,
                                            *mw, cnt_p, db)

    n_tok = bp * sp + db
    eidx = jnp.concatenate([e_p.reshape(-1, TOP_K), e_s.reshape(-1, TOP_K)], axis=0)
    rank = jnp.concatenate([r_p.reshape(-1, TOP_K), r_s.reshape(-1, TOP_K)], axis=0)
    counts = cnt.reshape(ne).astype(jnp.int32)
    pcounts = (counts + MOE_TILE - 1) // MOE_TILE * MOE_TILE
    pend = jnp.cumsum(pcounts)
    pstart = pend - pcounts
    dest = jnp.sum(jnp.where(eidx[..., None] == jnp.arange(ne, dtype=jnp.int32), pstart, 0), axis=-1) + rank
    nb = -(-(n_tok * TOP_K) // MOE_TILE) + ne
    nused = (pend[-1] // MOE_TILE).astype(jnp.int32)
    blk_start = jnp.minimum(jnp.arange(nb, dtype=jnp.int32), nused - 1) * MOE_TILE
    blk_e = jnp.sum(blk_start[:, None] >= pend[None, :], axis=-1).astype(jnp.int32)
    nvalid = jnp.clip((pstart + counts)[blk_e] - blk_start, 1, MOE_TILE).astype(jnp.int32)
    h2 = jnp.concatenate([h2_p, h2_s], axis=0)
    y = _experts(blk_e, nused.reshape(1), nvalid, dest.reshape(-1), h2, w_gate_up, b_gate_up, w_down, b_down)
    y_p = _combine(x1_p, g_p, y, n_tok, 0, math.gcd(n_tok, bp * sp, 256))
    y_s = _combine(x1_s, g_s, y, n_tok, bp * sp, db)

    def kv_leaf(kv, n):
        return kv.reshape(kv.shape[0], n, 2, HEADS, HEAD_DIM)

    g_n = s5_a_re.shape
    outs = dict(
        y_p=y_p, y_s=y_s.reshape(db, 1, d),
        kv_p=[kv_leaf(kv, keeps_p[g]) for g, kv in enumerate((kv0, kv1, kv2))],
        kv_s=[kv.reshape(db, 1, 2, HEADS, HEAD_DIM) for kv in (kvs0, kvs1, kvs2)],
        re_p=hr_p.reshape(bp, *g_n), im_p=hi_p.reshape(bp, *g_n),
        re_s=hr_s.reshape(db, *g_n), im_s=hi_s.reshape(db, *g_n))
    return outs


def kernel(x_prompt, x_sample, cache_kv_w128, cache_kv_w512, cache_kv_w2048, state_s5_re, state_s5_im, norm_mix, w_in, q_norm, k_norm, s5_a_re, s5_a_im, s5_log_step, s5_b_re, s5_b_im, s5_c_re, s5_c_im, s5_d, w_glu, b_glu, w_branch_attn, w_branch_s5, w_out, norm_ffn, w_router, b_router, w_gate_up, b_gate_up, w_down, b_down):
    depth = norm_mix.shape[0]
    xp, xs = x_prompt, x_sample
    acc = {k: [] for k in ("re_p", "im_p", "re_s", "im_s")}
    kv_p = [[] for _ in range(NG)]
    kv_s = [[] for _ in range(NG)]
    for l in range(depth):
        caches = (cache_kv_w128[l], cache_kv_w512[l], cache_kv_w2048[l])
        o = _layer(xp, xs, caches, state_s5_re[l], state_s5_im[l], norm_mix[l], w_in[l], q_norm[l], k_norm[l],
                   s5_a_re[l], s5_a_im[l], s5_log_step[l], s5_b_re[l], s5_b_im[l], s5_c_re[l], s5_c_im[l],
                   s5_d[l], w_glu[l], b_glu[l], w_branch_attn[l], w_branch_s5[l], w_out[l], norm_ffn[l],
                   w_router[l], b_router[l], w_gate_up[l], b_gate_up[l], w_down[l], b_down[l])
        xp, xs = o["y_p"], o["y_s"]
        for g in range(NG):
            kv_p[g].append(o["kv_p"][g])
            kv_s[g].append(o["kv_s"][g])
        for k in acc:
            acc[k].append(o[k])
    st = lambda xs_: jnp.stack(xs_)
    return (xp, xs, st(kv_p[0]), st(kv_p[1]), st(kv_p[2]), st(acc["re_p"]), st(acc["im_p"]),
            st(kv_s[0]), st(kv_s[1]), st(kv_s[2]), st(acc["re_s"]), st(acc["im_s"]))
```

```python
import functools
import math

import jax
import jax.numpy as jnp
from jax import lax
from jax.experimental import pallas as pl
from jax.experimental.pallas import tpu as pltpu

F32 = jnp.float32
BF16 = jnp.bfloat16

HEAD_DIM = 128
HEADS = 4
DIL_PATTERNS = ((128, 1), (512, 4), (2048, 16))
NG = len(DIL_PATTERNS)
QBLOCK = 128
ROPE_THETA = 10000.0
PAST_LEN = 8192
S5_GROUP_CH = 16
S5_STATE = 64
TOP_K = 4
SWIGLU_LIMIT = 7.0
SWIGLU_ALPHA = 1.702
RMS_EPS = 1e-6

LANES = 128
SUBLANES = 8
ATTN_ROWS = max(d for _, d in DIL_PATTERNS) * QBLOCK
S5_CHUNK_GROUPS = 16
MOE_TILE = 256
VMEM_LIMIT = 56 * 1024 * 1024


def _cparams(sem, vmem=VMEM_LIMIT):
    return pltpu.CompilerParams(dimension_semantics=sem, vmem_limit_bytes=vmem)


def _dot(a, b):
    return jnp.dot(a, b, preferred_element_type=F32)


def _dot_nt(a, b):
    return lax.dot_general(a, b, (((1,), (1,)), ((), ())), preferred_element_type=F32)


def _const_spec(shape, single=True):
    nd = len(shape)
    kw = dict(pipeline_mode=pl.Buffered(1)) if single else {}
    return pl.BlockSpec(shape, lambda *_: (0,) * nd, **kw)


def _inproj_kernel(x_ref, nw_ref, w_ref, qn_ref, kn_ref, cos_ref, sin_ref, *outs, head_major, kb, first_kv):
    if head_major:
        q_ref, k_ref, v_ref, kv0, kv1, kv2, u_ref, sa_ref, ss_ref = outs
    else:
        q_ref, kv0, kv1, kv2, u_ref, sa_ref, ss_ref = outs
    kvs = (kv0, kv1, kv2)
    x = x_ref[0]
    tm = x.shape[0]
    d = x.shape[1]
    h = x * lax.rsqrt(jnp.mean(x * x, axis=-1, keepdims=True) + RMS_EPS) * nw_ref[...]
    hb = h.astype(BF16)
    cos = cos_ref[...]
    sin = sin_ref[...]
    aw = NG * HEADS * HEAD_DIM
    gw = HEADS * HEAD_DIM

    def normrope(t, gvec):
        t = t * lax.rsqrt(jnp.mean(t * t, axis=-1, keepdims=True) + RMS_EPS) * gvec
        return t * cos + pltpu.roll(t, HEAD_DIM // 2, 1) * sin

    for g in range(NG):
        c0 = g * gw
        tq = _dot(hb, w_ref[:, c0:c0 + gw])
        tk = _dot(hb, w_ref[:, aw + c0:aw + c0 + gw])
        tv = _dot(hb, w_ref[:, 2 * aw + c0:2 * aw + c0 + gw])
        tails = []
        for hh in range(HEADS):
            sl = slice(hh * HEAD_DIM, (hh + 1) * HEAD_DIM)
            qh = normrope(tq[:, sl], qn_ref[g:g + 1, :])
            kh = normrope(tk[:, sl], kn_ref[g:g + 1, :])
            vh = tv[:, sl]
            if head_major:
                q_ref[0, g * HEADS + hh] = qh
                k_ref[0, g * HEADS + hh] = kh
                v_ref[0, g * HEADS + hh] = vh
            else:
                q_ref[0, :, c0 + hh * HEAD_DIM:c0 + (hh + 1) * HEAD_DIM] = qh
            tails.append((kh[tm - kb[g]:], vh[tm - kb[g]:]))

        @pl.when(pl.program_id(1) >= first_kv[g])
        def _(g=g, tails=tails):
            for hh, (kt, vt) in enumerate(tails):
                kvs[g][0, :, 0, hh, :] = kt
                kvs[g][0, :, 1, hh, :] = vt
    base = 3 * aw
    u_ref[0] = _dot(hb, w_ref[:, base:base + d])
    sa_ref[0] = jax.nn.sigmoid(_dot(hb, w_ref[:, base + d:base + 2 * d]))
    ss_ref[0] = jax.nn.sigmoid(_dot(hb, w_ref[:, base + 2 * d:base + 3 * d]))


def _inproj(x, norm_w, w_bf, qn, kn, cos2, sin2, keeps, head_major, tm):
    b, s, d = x.shape
    nt = s // tm
    in_w = w_bf.shape[1]
    nh = NG * HEADS
    gw = HEADS * HEAD_DIM
    kb = tuple(min(k, tm) for k in keeps)
    in_specs = [
        pl.BlockSpec((1, tm, d), lambda bi, i: (bi, i, 0)),
        _const_spec((1, d)),
        _const_spec((d, in_w)),
        _const_spec((NG, HEAD_DIM)),
        _const_spec((NG, HEAD_DIM)),
        pl.BlockSpec((tm, HEAD_DIM), lambda bi, i: (i, 0)),
        pl.BlockSpec((tm, HEAD_DIM), lambda bi, i: (i, 0)),
    ]
    hm_spec = pl.BlockSpec((1, nh, tm, HEAD_DIM), lambda bi, i: (bi, 0, i, 0))
    row_spec = pl.BlockSpec((1, tm, d), lambda bi, i: (bi, i, 0))
    out_shape, out_specs = [], []
    if head_major:
        for _ in range(3):
            out_shape.append(jax.ShapeDtypeStruct((b, nh, s, HEAD_DIM), F32))
            out_specs.append(hm_spec)
    else:
        out_shape.append(jax.ShapeDtypeStruct((b, s, NG * gw), F32))
        out_specs.append(pl.BlockSpec((1, tm, NG * gw), lambda bi, i: (bi, i, 0)))
    for g in range(NG):
        nkeep = keeps[g] // kb[g]
        out_shape.append(jax.ShapeDtypeStruct((b, keeps[g], 2, HEADS, HEAD_DIM), F32))
        out_specs.append(pl.BlockSpec(
            (1, kb[g], 2, HEADS, HEAD_DIM),
            lambda bi, i, off=nt - nkeep: (bi, jnp.maximum(i - off, 0), 0, 0, 0)))
    for _ in range(3):
        out_shape.append(jax.ShapeDtypeStruct((b, s, d), F32))
        out_specs.append(row_spec)
    return pl.pallas_call(
        functools.partial(_inproj_kernel, head_major=head_major, kb=kb,
                          first_kv=tuple(nt - keeps[g] // kb[g] for g in range(NG))),
        grid=(b, nt), in_specs=in_specs, out_specs=out_specs, out_shape=out_shape,
        compiler_params=_cparams(("arbitrary", "arbitrary")),
        name="inproj_hm" if head_major else "inproj_rm",
    )(x, norm_w, w_bf, qn, kn, cos2, sin2)


def _attn_prompt_kernel(*refs):
    q_refs = refs[0:3]
    kc_refs = refs[3:6]
    vc_refs = refs[6:9]
    kp_refs = refs[9:12]
    vp_refs = refs[12:15]
    out_ref, o_scr, l_scr = refs[15:18]
    n = pl.program_id(2)
    neg_first = jnp.where(n > 0, 0.0, -jnp.inf).astype(F32)
    row = lax.broadcasted_iota(jnp.int32, (QBLOCK, QBLOCK), 0)
    col = lax.broadcasted_iota(jnp.int32, (QBLOCK, QBLOCK), 1)
    mask_p = col >= row
    mask_c = col <= row
    scale = HEAD_DIM ** -0.5
    ones = jnp.ones((QBLOCK, HEAD_DIM), BF16)
    interleave = 4

    def combo(q, kp, vp, kc, vc, negp):
        qb = q.astype(BF16)
        sp = _dot_nt(qb, kp.astype(BF16)) * scale
        sc = _dot_nt(qb, kc.astype(BF16)) * scale
        sp = jnp.where(mask_p, sp, -jnp.inf) + negp
        sc = jnp.where(mask_c, sc, -jnp.inf)
        m = jnp.maximum(jnp.max(sp, axis=-1, keepdims=True), jnp.max(sc, axis=-1, keepdims=True))
        pp = jnp.exp(sp - m).astype(BF16)
        pc = jnp.exp(sc - m).astype(BF16)
        od = (_dot(pp, jnp.concatenate([vp.astype(BF16), ones], axis=1))
              + _dot(pc, jnp.concatenate([vc.astype(BF16), ones], axis=1)))
        den = od[:, HEAD_DIM:]
        return od[:, :HEAD_DIM] / den, m + jnp.log(den)

    for g, (window, dil) in enumerate(DIL_PATTERNS):
        assert window // dil == QBLOCK
        blk = dil * QBLOCK
        c = ATTN_ROWS // blk

        def rows(start, dil=dil):
            return pl.ds(start, QBLOCK) if dil == 1 else pl.ds(start, QBLOCK, stride=dil)

        def emit(start, o, lse, g=g, rows=rows):
            o_scr[g, rows(start), :] = o
            l_scr[g, rows(start), :] = jnp.broadcast_to(lse, (QBLOCK, HEAD_DIM))

        def first_body(r, carry, g=g, rows=rows, emit=emit):
            o, lse = combo(q_refs[g][0, 0, rows(r), :], kp_refs[g][0, 0, rows(r), :],
                           vp_refs[g][0, 0, rows(r), :], kc_refs[g][0, 0, rows(r), :],
                           vc_refs[g][0, 0, rows(r), :], neg_first)
            emit(r, o, lse)
            return carry

        lax.fori_loop(0, dil, first_body, 0, unroll=min(dil, interleave))

        def rest_body(idx, carry, g=g, rows=rows, emit=emit, dil=dil, blk=blk):
            start = (idx // dil + 1) * blk + idx % dil
            o, lse = combo(q_refs[g][0, 0, rows(start), :], kc_refs[g][0, 0, rows(start - blk), :],
                           vc_refs[g][0, 0, rows(start - blk), :], kc_refs[g][0, 0, rows(start), :],
                           vc_refs[g][0, 0, rows(start), :], 0.0)
            emit(start, o, lse)
            return carry

        if c > 1:
            lax.fori_loop(0, (c - 1) * dil, rest_body, 0, unroll=interleave)

    def mix_body(i, carry):
        rs = pl.ds(pl.multiple_of(i * QBLOCK, QBLOCK), QBLOCK)
        l0, l1, l2 = l_scr[0, rs, :], l_scr[1, rs, :], l_scr[2, rs, :]
        m = jnp.maximum(jnp.maximum(l0, l1), l2)
        e0, e1, e2 = jnp.exp(l0 - m), jnp.exp(l1 - m), jnp.exp(l2 - m)
        num = e0 * o_scr[0, rs, :] + e1 * o_scr[1, rs, :] + e2 * o_scr[2, rs, :]
        out_ref[0, 0, rs, :] = num / (e0 + e1 + e2)
        return carry

    lax.fori_loop(0, ATTN_ROWS // QBLOCK, mix_body, 0)


def _attn_prompt(q12, k12, v12):
    b, _, s, _ = q12.shape
    assert s % ATTN_ROWS == 0
    nb = s // ATTN_ROWS

    def cur_spec(g):
        return pl.BlockSpec((1, 1, ATTN_ROWS, HEAD_DIM), lambda bi, h, n, g=g: (bi, g * HEADS + h, n, 0))

    def prev_spec(g):
        blk = DIL_PATTERNS[g][1] * QBLOCK
        c = ATTN_ROWS // blk
        return pl.BlockSpec((1, 1, blk, HEAD_DIM),
                            lambda bi, h, n, g=g, c=c: (bi, g * HEADS + h, jnp.maximum(n * c - 1, 0), 0))

    in_specs = ([cur_spec(g) for g in range(NG)] * 3 + [prev_spec(g) for g in range(NG)] * 2)
    args = [q12] * 3 + [k12] * 3 + [v12] * 3 + [k12] * 3 + [v12] * 3
    return pl.pallas_call(
        _attn_prompt_kernel,
        grid=(b, HEADS, nb), in_specs=in_specs,
        out_specs=pl.BlockSpec((1, 1, ATTN_ROWS, HEAD_DIM), lambda bi, h, n: (bi, h, n, 0)),
        out_shape=jax.ShapeDtypeStruct((b, HEADS, s, HEAD_DIM), F32),
        scratch_shapes=[pltpu.VMEM((NG, ATTN_ROWS, HEAD_DIM), F32), pltpu.VMEM((NG, ATTN_ROWS, HEAD_DIM), F32)],
        compiler_params=_cparams(("arbitrary", "arbitrary", "arbitrary")),
        name="attn_prompt",
    )(*args)


def _attn_decode_kernel(q_ref, n0, n1, n2, c0, c1, c2, o_ref):
    scale = HEAD_DIM ** -0.5
    outs, lses = [], []
    for g, (n_ref, c_ref) in enumerate(((n0, c0), (n1, c1), (n2, c2))):
        q = q_ref[:, g]
        kn, vn = n_ref[:, 0], n_ref[:, 1]
        kc, vc = c_ref[:, :, 0], c_ref[:, :, 1]
        s_c = jnp.sum(kc * q[:, None], axis=-1, keepdims=True) * scale
        s_n = jnp.sum(kn * q, axis=-1, keepdims=True) * scale
        m = jnp.maximum(jnp.max(s_c, axis=1), s_n)
        p_c = jnp.exp(s_c - m[:, None])
        p_n = jnp.exp(s_n - m)
        den = jnp.sum(p_c, axis=1) + p_n
        outs.append((jnp.sum(p_c * vc, axis=1) + p_n * vn) / den)
        lses.append(m + jnp.log(den))
    m = jnp.maximum(jnp.maximum(lses[0], lses[1]), lses[2])
    es = [jnp.exp(l - m) for l in lses]
    o_ref[...] = (es[0] * outs[0] + es[1] * outs[1] + es[2] * outs[2]) / (es[0] + es[1] + es[2])


def _attn_decode(q4, kv_new, caches, bt=4):
    db = q4.shape[0]
    in_specs = [pl.BlockSpec((bt, NG, HEADS, HEAD_DIM), lambda i: (i, 0, 0, 0))]
    in_specs += [pl.BlockSpec((bt, 2, HEADS, HEAD_DIM), lambda i: (i, 0, 0, 0))] * NG
    cargs = []
    for g, (window, dil) in enumerate(DIL_PATTERNS):
        span = window // dil
        cache = caches[g]
        assert cache.shape[1] == span * dil, "decode path supports a full window buffer only"
        cargs.append(cache.reshape(db, span, dil, 2, HEADS, HEAD_DIM))
        in_specs.append(pl.BlockSpec((bt, span, None, 2, HEADS, HEAD_DIM), lambda i: (i, 0, 0, 0, 0, 0)))
    return pl.pallas_call(
        _attn_decode_kernel,
        grid=(db // bt,), in_specs=in_specs,
        out_specs=pl.BlockSpec((bt, HEADS, HEAD_DIM), lambda i: (i, 0, 0)),
        out_shape=jax.ShapeDtypeStruct((db, HEADS, HEAD_DIM), F32),
        compiler_params=_cparams(("arbitrary",)),
        name="attn_decode",
    )(q4, *kv_new, *cargs)


def _gelu(y):
    return 0.5 * y * (1.0 + jnp.tanh(math.sqrt(2.0 / math.pi) * (y + 0.044715 * (y * y * y))))


def _s5_prompt_kernel(u_ref, wbr_ref, wbi_ref, wcr_ref, wci_ref, ar_ref, ai_ref, d_ref,
                      z_ref, hr_ref, hi_ref, sre, sim, st_re, st_im):
    t = pl.program_id(1)
    nb, tc, _ = u_ref.shape
    nj = sre.shape[1] // tc

    @pl.when(t == 0)
    def _():
        st_re[...] = jnp.zeros_like(st_re)
        st_im[...] = jnp.zeros_like(st_im)

    for b in range(nb):
        ub = u_ref[b].astype(BF16)
        bre = _dot(ub, wbr_ref[0])
        bim = _dot(ub, wbi_ref[0])
        for j in range(nj):
            sre[b, pl.ds(j, tc, stride=nj), :] = bre[:, j * LANES:(j + 1) * LANES]
            sim[b, pl.ds(j, tc, stride=nj), :] = bim[:, j * LANES:(j + 1) * LANES]

    ar = ar_ref[0]
    ai = ai_ref[0]

    def step(i, carry):
        rs = pl.ds(pl.multiple_of(i * nj, nj), nj)
        new = []
        for b in range(nb):
            hr, hi = carry[2 * b], carry[2 * b + 1]
            nr = ar * hr - ai * hi + sre[b, rs, :]
            ni = ar * hi + ai * hr + sim[b, rs, :]
            sre[b, rs, :] = nr
            sim[b, rs, :] = ni
            new += [nr, ni]
        return tuple(new)

    init = []
    for b in range(nb):
        init += [st_re[b], st_im[b]]
    fin = lax.fori_loop(0, tc, step, tuple(init), unroll=2)
    for b in range(nb):
        st_re[b] = fin[2 * b]
        st_im[b] = fin[2 * b + 1]
        hr_ref[b, 0] = fin[2 * b]
        hi_ref[b, 0] = fin[2 * b + 1]

    for b in range(nb):
        hre = jnp.concatenate([sre[b, pl.ds(j, tc, stride=nj), :] for j in range(nj)], axis=1).astype(BF16)
        him = jnp.concatenate([sim[b, pl.ds(j, tc, stride=nj), :] for j in range(nj)], axis=1).astype(BF16)
        y = _dot(hre, wcr_ref[0]) + _dot(him, wci_ref[0]) + d_ref[0] * u_ref[b]
        z_ref[b] = _gelu(y)


def _s5_prompt(u, wbr, wbi, wcr, wci, ar, ai, dsk, tc=512):
    b, s, w = u.shape
    nc, cw, fw = wbr.shape
    nj = fw // LANES
    assert nj == SUBLANES and s % tc == 0
    wspec = lambda shp: pl.BlockSpec((1,) + shp, lambda c, t: (c, 0, 0))
    st_shape = jax.ShapeDtypeStruct((b, nc, nj, LANES), F32)
    st_spec = pl.BlockSpec((b, 1, nj, LANES), lambda c, t: (0, c, 0, 0))
    return pl.pallas_call(
        _s5_prompt_kernel,
        grid=(nc, s // tc),
        in_specs=[pl.BlockSpec((b, tc, cw), lambda c, t: (0, t, c)),
                  wspec((cw, fw)), wspec((cw, fw)), wspec((fw, cw)), wspec((fw, cw)),
                  wspec((nj, LANES)), wspec((nj, LANES)), wspec((1, cw))],
        out_specs=[pl.BlockSpec((b, tc, cw), lambda c, t: (0, t, c)), st_spec, st_spec],
        out_shape=[jax.ShapeDtypeStruct((b, s, w), F32), st_shape, st_shape],
        scratch_shapes=[pltpu.VMEM((b, tc * nj, LANES), F32), pltpu.VMEM((b, tc * nj, LANES), F32),
                        pltpu.VMEM((b, nj, LANES), F32), pltpu.VMEM((b, nj, LANES), F32)],
        compiler_params=_cparams(("arbitrary", "arbitrary")),
        name="s5_prompt",
    )(u, wbr, wbi, wcr, wci, ar, ai, dsk)


def _s5_step_kernel(u_ref, h0r_ref, h0i_ref, wbr_ref, wbi_ref, wcr_ref, wci_ref, ar_ref, ai_ref, d_ref,
                    z_ref, hr_ref, hi_ref):
    nc, cw, fw = wbr_ref.shape
    for c in range(nc):
        cs = slice(c * cw, (c + 1) * cw)
        fs = slice(c * fw, (c + 1) * fw)
        uc = u_ref[:, cs]
        ub = uc.astype(BF16)
        ar, ai = ar_ref[:, fs], ai_ref[:, fs]
        h0r, h0i = h0r_ref[:, fs], h0i_ref[:, fs]
        hr = ar * h0r - ai * h0i + _dot(ub, wbr_ref[c])
        hi = ar * h0i + ai * h0r + _dot(ub, wbi_ref[c])
        hr_ref[:, fs] = hr
        hi_ref[:, fs] = hi
        y = _dot(hr.astype(BF16), wcr_ref[c]) + _dot(hi.astype(BF16), wci_ref[c]) + d_ref[:, cs] * uc
        z_ref[:, cs] = _gelu(y)


def _s5_step(u, h0r, h0i, wbr, wbi, wcr, wci, ar, ai, dsk):
    db, w = u.shape
    nf = h0r.shape[1]
    shapes = [jax.ShapeDtypeStruct((db, w), F32), jax.ShapeDtypeStruct((db, nf), F32),
              jax.ShapeDtypeStruct((db, nf), F32)]
    return pl.pallas_call(
        _s5_step_kernel, out_shape=shapes, compiler_params=_cparams(None), name="s5_step",
    )(u, h0r, h0i, wbr, wbi, wcr, wci, ar, ai, dsk)


def _s5_params(a_re, a_im, log_step, b_re, b_im, c_re, c_im, d_skip):
    g, n = a_re.shape
    ch = b_re.shape[2]
    cg = S5_CHUNK_GROUPS
    nc = g // cg
    step = jnp.exp(log_step)[:, None]
    mag = jnp.exp(a_re * step)
    abr = mag * jnp.cos(a_im * step)
    abi = mag * jnp.sin(a_im * step)
    nr, ni = abr - 1.0, abi
    den = a_re * a_re + a_im * a_im
    qr = (nr * a_re + ni * a_im) / den
    qi = (ni * a_re - nr * a_im) / den
    bbr = qr[..., None] * b_re - qi[..., None] * b_im
    bbi = qr[..., None] * b_im + qi[..., None] * b_re
    eye = jnp.eye(cg, dtype=F32)

    def in_mat(bb):
        t = bb.reshape(nc, cg, n, ch).transpose(0, 1, 3, 2)
        return jnp.einsum("kgcn,gh->kgchn", t, eye).reshape(nc, cg * ch, cg * n).astype(BF16)

    def out_mat(cc):
        t = cc.reshape(nc, cg, ch, n).transpose(0, 1, 3, 2)
        return jnp.einsum("kgnc,gh->kgnhc", t, eye).reshape(nc, cg * n, cg * ch).astype(BF16)

    return (in_mat(bbr), in_mat(bbi), out_mat(c_re), out_mat(-c_im),
            abr.reshape(nc, cg * n), abi.reshape(nc, cg * n), d_skip.reshape(nc, 1, cg * ch))


def _merge_kernel(x_ref, at_ref, z_ref, sa_ref, ss_ref, wglu_ref, bglu_ref, wba_ref, wbs_ref, wout_ref,
                  nf_ref, wr_ref, br_ref, base_ref,
                  x1_ref, h2_ref, eidx_ref, gate_ref, rank_ref, cnt_ref, cnt_scr):
    first = jnp.logical_and(pl.program_id(0) == 0, pl.program_id(1) == 0)

    @pl.when(first)
    def _():
        cnt_scr[...] = base_ref[...]

    x = x_ref[0]
    tm = x.shape[0]
    attn = jnp.concatenate([at_ref[0, hh] for hh in range(HEADS)], axis=1).astype(BF16)
    z = z_ref[0]
    zb = z.astype(BF16)
    s5o = z * jax.nn.sigmoid(_dot(zb, wglu_ref[...]) + bglu_ref[...])
    merged = sa_ref[0] * _dot(attn, wba_ref[...]) + ss_ref[0] * _dot(s5o.astype(BF16), wbs_ref[...])
    x1 = x + _dot(merged.astype(BF16), wout_ref[...])
    x1_ref[0] = x1
    h2 = x1 * lax.rsqrt(jnp.mean(x1 * x1, axis=-1, keepdims=True) + RMS_EPS) * nf_ref[...]
    for j in range(h2.shape[1] // LANES):
        h2_ref[pl.ds(j, tm, stride=SUBLANES), :] = h2[:, j * LANES:(j + 1) * LANES]

    logits = _dot(h2.astype(BF16), wr_ref[...]) + br_ref[...]
    ne = logits.shape[1]
    lane = lax.broadcasted_iota(jnp.int32, (tm, ne), 1)
    work = logits
    vals, idxs, sels = [], [], []
    for _ in range(TOP_K):
        m = jnp.max(work, axis=-1, keepdims=True)
        idx = jnp.min(jnp.where(work == m, lane, ne), axis=-1, keepdims=True)
        sel = lane == idx
        vals.append(m)
        idxs.append(idx)
        sels.append(sel)
        work = jnp.where(sel, -jnp.inf, work)
    es = [jnp.exp(v - vals[0]) for v in vals]
    den = es[0] + es[1] + es[2] + es[3]
    member = jnp.zeros((tm, ne), F32)
    for sel in sels:
        member = member + sel.astype(F32)
    r = lax.broadcasted_iota(jnp.int32, (tm, tm), 0)
    c = lax.broadcasted_iota(jnp.int32, (tm, tm), 1)
    ltri = (c < r).astype(BF16)
    tot = cnt_scr[...] + _dot(ltri, member.astype(BF16))
    k4 = lax.broadcasted_iota(jnp.int32, (tm, TOP_K), 1)
    eidx = jnp.zeros((tm, TOP_K), jnp.int32)
    gate = jnp.zeros((tm, TOP_K), F32)
    rank = jnp.zeros((tm, TOP_K), jnp.int32)
    for k in range(TOP_K):
        rk = jnp.sum(jnp.where(sels[k], tot, 0.0), axis=-1, keepdims=True).astype(jnp.int32)
        eidx = jnp.where(k4 == k, idxs[k], eidx)
        gate = jnp.where(k4 == k, es[k] / den, gate)
        rank = jnp.where(k4 == k, rk, rank)
    eidx_ref[0] = eidx
    gate_ref[0] = gate
    rank_ref[0] = rank
    cnt_scr[...] = cnt_scr[...] + jnp.sum(member, axis=0, keepdims=True)
    cnt_ref[...] = cnt_scr[...]


def _merge(x, attn_hm, z, sa, ss, wglu, bglu, wba, wbs, wout, nf, wr, br, base_cnt, tm):
    b, s, d = x.shape
    nt = s // tm
    ne = wr.shape[1]
    row = pl.BlockSpec((1, tm, d), lambda bi, i: (bi, i, 0))
    k_spec = pl.BlockSpec((1, tm, TOP_K), lambda bi, i: (bi, i, 0))
    in_specs = [row, pl.BlockSpec((1, HEADS, tm, HEAD_DIM), lambda bi, i: (bi, 0, i, 0)), row, row, row,
                _const_spec(wglu.shape), _const_spec(bglu.shape), _const_spec(wba.shape), _const_spec(wbs.shape),
                _const_spec(wout.shape), _const_spec(nf.shape), _const_spec(wr.shape), _const_spec(br.shape),
                _const_spec(base_cnt.shape)]
    out_shape = [jax.ShapeDtypeStruct((b, s, d), F32),
                 jax.ShapeDtypeStruct((b * s * SUBLANES, LANES), F32),
                 jax.ShapeDtypeStruct((b, s, TOP_K), jnp.int32),
                 jax.ShapeDtypeStruct((b, s, TOP_K), F32),
                 jax.ShapeDtypeStruct((b, s, TOP_K), jnp.int32),
                 jax.ShapeDtypeStruct((1, ne), F32)]
    out_specs = [row, pl.BlockSpec((tm * SUBLANES, LANES), lambda bi, i: (bi * nt + i, 0)),
                 k_spec, k_spec, k_spec, pl.BlockSpec((1, ne), lambda bi, i: (0, 0))]
    return pl.pallas_call(
        _merge_kernel, grid=(b, nt), in_specs=in_specs, out_specs=out_specs, out_shape=out_shape,
        scratch_shapes=[pltpu.VMEM((1, ne), F32)],
        compiler_params=_cparams(("arbitrary", "arbitrary")),
        name="merge_router",
    )(x, attn_hm, z, sa, ss, wglu, bglu, wba, wbs, wout, nf, wr, br, base_cnt)


def _expert_kernel(blk_e_ref, nused_ref, nvalid_ref, dest_ref, h2_hbm, wgu_ref, bgu_ref, wd_ref, bd_ref, y_hbm,
                   src, xbuf, ybuf, wgu_bf, wd_bf, gsem, ssem, *, n_tok):
    b = pl.program_id(0)
    nused = nused_ref[0]
    tmx = ybuf.shape[1] // SUBLANES
    d = wgu_bf.shape[0]
    unroll = SUBLANES

    def issue_gather(blk, slot):
        last = nvalid_ref[blk] - 1

        def body(c, carry):
            for u in range(unroll):
                i = c * unroll + u
                s = src[blk * tmx + jnp.minimum(i, last)]
                pltpu.make_async_copy(h2_hbm.at[pl.ds((s >> 2) * SUBLANES, SUBLANES), :],
                                      xbuf.at[slot, pl.ds(i * SUBLANES, SUBLANES), :],
                                      gsem.at[slot]).start(priority=u % 2)
            return carry
        lax.fori_loop(0, tmx // unroll, body, 0)

    def wait_scatter(slot, n):
        rows = pl.ds(0, n * SUBLANES)
        pltpu.make_async_copy(ybuf.at[slot, rows, :], y_hbm.at[rows, :], ssem.at[0]).wait()

    @pl.when(b < nused)
    def _():
        slot = b % 2

        @pl.when(b == 0)
        def _():
            def inv(a, carry):
                src[dest_ref[a]] = a
                return carry
            lax.fori_loop(0, n_tok * TOP_K, inv, 0, unroll=unroll)
            issue_gather(0, 0)

        @pl.when(b + 1 < nused)
        def _():
            issue_gather(b + 1, 1 - slot)

        pltpu.make_async_copy(h2_hbm.at[pl.ds(0, tmx * SUBLANES), :], xbuf.at[slot], gsem.at[slot]).wait()

        new_expert = jnp.logical_or(b == 0, blk_e_ref[b] != blk_e_ref[jnp.maximum(b - 1, 0)])

        @pl.when(new_expert)
        def _():
            wgu_bf[...] = wgu_ref[0].astype(BF16)
            wd_bf[...] = wd_ref[0].astype(BF16)

        x = jnp.concatenate([xbuf[slot, pl.ds(j, tmx, stride=SUBLANES), :] for j in range(d // LANES)],
                            axis=1).astype(BF16)
        gu = _dot(x, wgu_bf[...]) + bgu_ref[0]
        de = gu.shape[1] // 2
        gl = jnp.minimum(gu[:, :de], SWIGLU_LIMIT)
        lin = jnp.clip(gu[:, de:], -SWIGLU_LIMIT, SWIGLU_LIMIT)
        act = gl * jax.nn.sigmoid(SWIGLU_ALPHA * gl) * (lin + 1.0)
        y = _dot(act.astype(BF16), wd_bf[...]) + bd_ref[0]
        for j in range(d // LANES):
            ybuf[slot, pl.ds(j, tmx, stride=SUBLANES), :] = y[:, j * LANES:(j + 1) * LANES]

        @pl.when(b > 0)
        def _():
            wait_scatter(1 - slot, nvalid_ref[jnp.maximum(b - 1, 0)])

        nv = nvalid_ref[b]

        def issue_s(i, priority):
            s = src[b * tmx + i]
            pos = (s & (TOP_K - 1)) * n_tok + (s >> 2)
            pltpu.make_async_copy(ybuf.at[slot, pl.ds(i * SUBLANES, SUBLANES), :],
                                  y_hbm.at[pl.ds(pos * SUBLANES, SUBLANES), :],
                                  ssem.at[0]).start(priority=priority)

        def issue_group(c, carry):
            for u in range(unroll):
                issue_s(c * unroll + u, u % 2)
            return carry

        def issue_one(i, carry):
            issue_s(i, 0)
            return carry

        nfull = nv // unroll
        lax.fori_loop(0, nfull, issue_group, 0)
        lax.fori_loop(nfull * unroll, nv, issue_one, 0)

        @pl.when(b == nused - 1)
        def _():
            wait_scatter(slot, nv)


def _experts(blk_e, nused, nvalid, dest, h2, wgu, bgu, wd, bd):
    assert TOP_K == 4
    ne, d, de2 = wgu.shape
    nb = blk_e.shape[0]
    n_tok = dest.shape[0] // TOP_K
    tmx = MOE_TILE
    wmap = lambda b, e, *_: (e[b], 0, 0)
    grid_spec = pltpu.PrefetchScalarGridSpec(
        num_scalar_prefetch=4, grid=(nb,),
        in_specs=[pl.BlockSpec(memory_space=pl.ANY),
                  pl.BlockSpec((1, d, de2), wmap), pl.BlockSpec((1, 1, de2), wmap),
                  pl.BlockSpec((1, de2 // 2, d), wmap), pl.BlockSpec((1, 1, d), wmap)],
        out_specs=pl.BlockSpec(memory_space=pl.ANY),
        scratch_shapes=[pltpu.SMEM((nb * tmx,), jnp.int32),
                        pltpu.VMEM((2, tmx * SUBLANES, LANES), F32), pltpu.VMEM((2, tmx * SUBLANES, LANES), F32),
                        pltpu.VMEM((d, de2), BF16), pltpu.VMEM((de2 // 2, d), BF16),
                        pltpu.SemaphoreType.DMA((2,)), pltpu.SemaphoreType.DMA((1,))])
    return pl.pallas_call(
        functools.partial(_expert_kernel, n_tok=n_tok), grid_spec=grid_spec,
        out_shape=jax.ShapeDtypeStruct((TOP_K * n_tok * SUBLANES, LANES), F32),
        compiler_params=_cparams(("arbitrary",)),
        name="experts",
    )(blk_e, nused, nvalid, dest, h2, wgu, bgu.reshape(ne, 1, de2), wd, bd.reshape(ne, 1, d))


def _combine_kernel(x1_ref, gate_ref, *refs):
    y_refs, o_ref = refs[:TOP_K], refs[TOP_K]
    tc = x1_ref.shape[1]
    gate = gate_ref[0]
    for j in range(x1_ref.shape[2] // LANES):
        acc = x1_ref[0, :, j * LANES:(j + 1) * LANES]
        for k in range(TOP_K):
            acc = acc + gate[:, k:k + 1] * y_refs[k][pl.ds(j, tc, stride=SUBLANES), :]
        o_ref[0, :, j * LANES:(j + 1) * LANES] = acc


def _combine(x1, gate, y, n_slot, tok_off, tc):
    b, s, d = x1.shape
    nt = s // tc
    assert n_slot % tc == 0 and tok_off % tc == 0
    y_specs = [pl.BlockSpec((tc * SUBLANES, LANES),
                            lambda bi, i, base=(k * n_slot + tok_off) // tc: (base + bi * nt + i, 0))
               for k in range(TOP_K)]
    return pl.pallas_call(
        _combine_kernel, grid=(b, nt),
        in_specs=[pl.BlockSpec((1, tc, d), lambda bi, i: (bi, i, 0)),
                  pl.BlockSpec((1, tc, TOP_K), lambda bi, i: (bi, i, 0))] + y_specs,
        out_specs=pl.BlockSpec((1, tc, d), lambda bi, i: (bi, i, 0)),
        out_shape=jax.ShapeDtypeStruct((b, s, d), F32),
        compiler_params=_cparams(("arbitrary", "arbitrary")),
        name="combine",
    )(x1, gate, *([y] * TOP_K))


def _rope_tables(pos):
    half = HEAD_DIM // 2
    freq = ROPE_THETA ** (-jnp.arange(half, dtype=F32) / half)
    ang = pos.astype(F32)[:, None] * freq[None, :]
    cos, sin = jnp.cos(ang), jnp.sin(ang)
    return jnp.concatenate([cos, cos], axis=-1), jnp.concatenate([-sin, sin], axis=-1)


def _layer(xp, xs, caches, h0r, h0i, norm_mix, w_in, q_norm, k_norm, s5_a_re, s5_a_im, s5_log_step,
           s5_b_re, s5_b_im, s5_c_re, s5_c_im, s5_d, w_glu, b_glu, w_branch_attn, w_branch_s5, w_out,
           norm_ffn, w_router, b_router, w_gate_up, b_gate_up, w_down, b_down):
    bp, sp, d = xp.shape
    db, ds, _ = xs.shape
    assert ds == 1, "decode path handles one new token per sequence"
    ne = w_router.shape[1]
    gw = HEADS * HEAD_DIM
    w_in_bf = w_in.astype(BF16)
    nm = norm_mix.reshape(1, d)
    s5p = _s5_params(s5_a_re, s5_a_im, s5_log_step, s5_b_re, s5_b_im, s5_c_re, s5_c_im, s5_d)
    wbr, wbi, wcr, wci, abr, abi, dsk = s5p
    nc = wbr.shape[0]

    cos_p, sin_p = _rope_tables(jnp.arange(sp, dtype=jnp.int32))
    keeps_p = tuple(min(w, sp) for w, _ in DIL_PATTERNS)
    q12, k12, v12, kv0, kv1, kv2, u_p, sa_p, ss_p = _inproj(
        xp, nm, w_in_bf, q_norm, k_norm, cos_p, sin_p, keeps_p, True, 256)
    attn_p = _attn_prompt(q12, k12, v12)
    z_p, hr_p, hi_p = _s5_prompt(u_p, wbr, wbi, wcr, wci, abr.reshape(nc, SUBLANES, LANES),
                                 abi.reshape(nc, SUBLANES, LANES), dsk)

    cos_s, sin_s = _rope_tables(jnp.full((db,), PAST_LEN, jnp.int32))
    q_s, kvs0, kvs1, kvs2, u_s, sa_s, ss_s = _inproj(
        xs.reshape(1, db, d), nm, w_in_bf, q_norm, k_norm, cos_s, sin_s, (db,) * NG, False, db)
    kv_new = [kv.reshape(db, 2, HEADS, HEAD_DIM) for kv in (kvs0, kvs1, kvs2)]
    attn_s = _attn_decode(q_s.reshape(db, NG, HEADS, HEAD_DIM), kv_new, caches)
    attn_s = attn_s.transpose(1, 0, 2).reshape(1, HEADS, db, HEAD_DIM)
    z_s, hr_s, hi_s = _s5_step(u_s.reshape(db, d), h0r.reshape(db, -1), h0i.reshape(db, -1), wbr, wbi, wcr, wci,
                               abr.reshape(1, -1), abi.reshape(1, -1), dsk.reshape(1, -1))

    mw = (w_glu.astype(BF16), b_glu.reshape(1, d), w_branch_attn.astype(BF16), w_branch_s5.astype(BF16),
          w_out.astype(BF16), norm_ffn.reshape(1, d), w_router.astype(BF16), b_router.reshape(1, ne))
    x1_p, h2_p, e_p, g_p, r_p, cnt_p = _merge(xp, attn_p, z_p, sa_p, ss_p, *mw, jnp.zeros((1, ne), F32), 256)
    x1_s, h2_s, e_s, g_s, r_s, cnt = _merge(xs.reshape(1, db, d), attn_s, z_s.reshape(1, db, d), sa_s, ss_s,
                                            *mw, cnt_p, db)

    n_tok = bp * sp + db
    eidx = jnp.concatenate([e_p.reshape(-1, TOP_K), e_s.reshape(-1, TOP_K)], axis=0)
    rank = jnp.concatenate([r_p.reshape(-1, TOP_K), r_s.reshape(-1, TOP_K)], axis=0)
    counts = cnt.reshape(ne).astype(jnp.int32)
    pcounts = (counts + MOE_TILE - 1) // MOE_TILE * MOE_TILE
    pend = jnp.cumsum(pcounts)
    pstart = pend - pcounts
    dest = jnp.sum(jnp.where(eidx[..., None] == jnp.arange(ne, dtype=jnp.int32), pstart, 0), axis=-1) + rank
    nb = -(-(n_tok * TOP_K) // MOE_TILE) + ne
    nused = (pend[-1] // MOE_TILE).astype(jnp.int32)
    blk_start = jnp.minimum(jnp.arange(nb, dtype=jnp.int32), nused - 1) * MOE_TILE
    blk_e = jnp.sum(blk_start[:, None] >= pend[None, :], axis=-1).astype(jnp.int32)
    nvalid = jnp.clip((pstart + counts)[blk_e] - blk_start, 1, MOE_TILE).astype(jnp.int32)
    h2 = jnp.concatenate([h2_p, h2_s], axis=0)
    y = _experts(blk_e, nused.reshape(1), nvalid, dest.reshape(-1), h2, w_gate_up, b_gate_up, w_down, b_down)
    y_p = _combine(x1_p, g_p, y, n_tok, 0, math.gcd(n_tok, bp * sp, 256))
    y_s = _combine(x1_s, g_s, y, n_tok, bp * sp, db)

    def kv_leaf(kv, n):
        return kv.reshape(kv.shape[0], n, 2, HEADS, HEAD_DIM)

    g_n = s5_a_re.shape
    outs = dict(
        y_p=y_p, y_s=y_s.reshape(db, 1, d),
        kv_p=[kv_leaf(kv, keeps_p[g]) for g, kv in enumerate((kv0, kv1, kv2))],
        kv_s=[kv.reshape(db, 1, 2, HEADS, HEAD_DIM) for kv in (kvs0, kvs1, kvs2)],
        re_p=hr_p.reshape(bp, *g_n), im_p=hi_p.reshape(bp, *g_n),
        re_s=hr_s.reshape(db, *g_n), im_s=hi_s.reshape(db, *g_n))
    return outs


def kernel(x_prompt, x_sample, cache_kv_w128, cache_kv_w512, cache_kv_w2048, state_s5_re, state_s5_im, norm_mix, w_in, q_norm, k_norm, s5_a_re, s5_a_im, s5_log_step, s5_b_re, s5_b_im, s5_c_re, s5_c_im, s5_d, w_glu, b_glu, w_branch_attn, w_branch_s5, w_out, norm_ffn, w_router, b_router, w_gate_up, b_gate_up, w_down, b_down):
    depth = norm_mix.shape[0]
    xp, xs = x_prompt, x_sample
    acc = {k: [] for k in ("re_p", "im_p", "re_s", "im_s")}
    kv_p = [[] for _ in range(NG)]
    kv_s = [[] for _ in range(NG)]
    for l in range(depth):
        caches = (cache_kv_w128[l], cache_kv_w512[l], cache_kv_w2048[l])
        o = _layer(xp, xs, caches, state_s5_re[l], state_s5_im[l], norm_mix[l], w_in[l], q_norm[l], k_norm[l],
                   s5_a_re[l], s5_a_im[l], s5_log_step[l], s5_b_re[l], s5_b_im[l], s5_c_re[l], s5_c_im[l],
                   s5_d[l], w_glu[l], b_glu[l], w_branch_attn[l], w_branch_s5[l], w_out[l], norm_ffn[l],
                   w_router[l], b_router[l], w_gate_up[l], b_gate_up[l], w_down[l], b_down[l])
        xp, xs = o["y_p"], o["y_s"]
        for g in range(NG):
            kv_p[g].append(o["kv_p"][g])
            kv_s[g].append(o["kv_s"][g])
        for k in acc:
            acc[k].append(o[k])
    st = lambda xs_: jnp.stack(xs_)
    return (xp, xs, st(kv_p[0]), st(kv_p[1]), st(kv_p[2]), st(acc["re_p"]), st(acc["im_p"]),
            st(kv_s[0]), st(kv_s[1]), st(kv_s[2]), st(acc["re_s"]), st(acc["im_s"]))
```
